```python
import math
import jax, jax.numpy as jnp
from jax import lax
import numpy as np

D_MODEL = 2048
BATCH = 4
SEQ = 2048
DEPTH = 4

M_HEADS = 8
M_QK = D_MODEL // 16
M_V = D_MODEL // 8
M_WIDTH = M_HEADS * M_V
M_CHUNK = 64
A_HEADS = 8
A_QK = D_MODEL // 16
A_V = 2 * A_QK
A_WIDTH = A_HEADS * A_V
ROPE_DIM = A_QK // 4
ROPE_THETA = 500000.0
Q_BLOCK = 128
D_FF = ((8 * D_MODEL // 3 + 255) // 256) * 256
CONV_W = 3
EPS = 1e-6
N_BRANCH = 2

SPLITS = [M_HEADS * M_QK, M_HEADS * M_QK, M_WIDTH, M_WIDTH, M_HEADS, M_HEADS,
          2 * A_HEADS * A_QK, 2 * A_HEADS * A_QK, A_WIDTH, N_BRANCH * D_MODEL]
OFFSETS = [int(o) for o in np.cumsum(SPLITS)[:-1]]
N_IN = int(sum(SPLITS))
F_GATE_OFF = OFFSETS[4]

kernel_name = "hybrid_mlstm_diffattn_convglu"


def rms_norm(x, g):
    xf = x.astype(jnp.float32)
    y = xf * lax.rsqrt(jnp.mean(xf * xf, axis=-1, keepdims=True) + EPS)
    return (y * g.astype(jnp.float32)).astype(x.dtype)


def partial_rotary(x, pos):
    half = ROPE_DIM // 2
    inv = ROPE_THETA ** (-jnp.arange(half, dtype=jnp.float32) / half)
    ang = pos.astype(jnp.float32)[:, None] * inv[None, :]
    cos = jnp.cos(ang)[None, :, None, None, :]
    sin = jnp.sin(ang)[None, :, None, None, :]
    xr = x[..., :ROPE_DIM].astype(jnp.float32)
    x1, x2 = xr[..., :half], xr[..., half:]
    rot = jnp.concatenate([x1 * cos - x2 * sin, x2 * cos + x1 * sin], axis=-1)
    return jnp.concatenate([rot.astype(x.dtype), x[..., ROPE_DIM:]], axis=-1)


def mlstm_chunkwise(q, k, v, i_pre, f_pre):
    B, H, S, dk = q.shape
    dv = v.shape[-1]
    L = M_CHUNK
    NC = S // L
    f32 = jnp.float32
    q = q.astype(f32)
    k = k.astype(f32) * (dk ** -0.5)
    v = v.astype(f32)
    log_f = jax.nn.log_sigmoid(f_pre.astype(f32))
    i_log = i_pre.astype(f32)

    def to_chunks(a):
        return jnp.moveaxis(a.reshape(B, H, NC, L, *a.shape[3:]), 2, 0)

    qc, kc, vc, ic = to_chunks(q), to_chunks(k), to_chunks(v), to_chunks(i_log)
    bc = jnp.cumsum(to_chunks(log_f), axis=-1)
    causal = jnp.tril(jnp.ones((L, L), dtype=bool))

    def step(carry, xs):
        C, n, m = carry
        q_, k_, v_, i_, b_ = xs
        log_d = b_[..., :, None] - b_[..., None, :] + i_[..., None, :]
        log_d = jnp.where(causal, log_d, -jnp.inf)
        inter_log = b_ + m[..., None]
        m_t = jnp.maximum(inter_log, jnp.max(log_d, axis=-1))
        d_mat = jnp.where(causal, jnp.exp(log_d - m_t[..., None]), 0.0)
        inter = jnp.exp(inter_log - m_t)
        s_mat = jnp.einsum('bhtd,bhsd->bhts', q_, k_) * d_mat
        num = (jnp.einsum('bhts,bhsv->bhtv', s_mat, v_)
               + inter[..., None] * jnp.einsum('bhtd,bhvd->bhtv', q_, C))
        den = s_mat.sum(-1) + inter * jnp.einsum('bhtd,bhd->bht', q_, n)
        h = num / jnp.maximum(jnp.abs(den), jnp.exp(-m_t))[..., None]
        g = b_[..., -1]
        w_log = g[..., None] - b_ + i_
        m_new = jnp.maximum(g + m, jnp.max(w_log, axis=-1))
        decay = jnp.exp(g + m - m_new)
        w = jnp.exp(w_log - m_new[..., None])
        C_new = decay[..., None, None] * C + jnp.einsum('bhs,bhsv,bhsd->bhvd', w, v_, k_)
        n_new = decay[..., None] * n + jnp.einsum('bhs,bhsd->bhd', w, k_)
        return (C_new, n_new, m_new), h

    init = (jnp.zeros((B, H, dv, dk), f32), jnp.zeros((B, H, dk), f32), jnp.zeros((B, H), f32))
    _, hs = lax.scan(step, init, (qc, kc, vc, ic, bc))
    return jnp.moveaxis(hs, 0, 2).reshape(B, H, S, dv)


def diff_attention(q, k, v, lam):
    B, H, _, S, d = q.shape
    dv = v.shape[-1]
    nb = S // Q_BLOCK
    scale = d ** -0.5
    kpos = jnp.arange(S)

    def block(i):
        start = i * Q_BLOCK
        qb = lax.dynamic_slice_in_dim(q, start, Q_BLOCK, axis=3)
        s = jnp.einsum('bhcqd,bhckd->bhcqk', qb, k).astype(jnp.float32) * scale
        qpos = start + jnp.arange(Q_BLOCK)
        s = jnp.where(kpos[None, :] <= qpos[:, None], s, -jnp.inf)
        p = jax.nn.softmax(s, axis=-1)
        a = p[:, :, 0] - lam * p[:, :, 1]
        return jnp.einsum('bhqk,bhkv->bhqv', a.astype(v.dtype), v)

    out = lax.map(block, jnp.arange(nb))
    return jnp.moveaxis(out, 0, 2).reshape(B, H, S, dv)


def hybrid_mixer(h, pos, layer, w_in, b_in, m_norm, a_norm, lam_qk, w_bm, w_bd, w_o):
    B, S, _ = h.shape
    z = h @ w_in + b_in
    mq, mk, mv, mo, mi, mf, aq, ak, av, gt = jnp.split(z, OFFSETS, axis=-1)

    q = mq.reshape(B, S, M_HEADS, M_QK).transpose(0, 2, 1, 3)
    k = mk.reshape(B, S, M_HEADS, M_QK).transpose(0, 2, 1, 3)
    v = mv.reshape(B, S, M_HEADS, M_V).transpose(0, 2, 1, 3)
    hm = mlstm_chunkwise(q, k, v, mi.transpose(0, 2, 1), mf.transpose(0, 2, 1))
    hm = hm.transpose(0, 2, 1, 3).astype(h.dtype)
    hm = rms_norm(hm, m_norm.reshape(M_HEADS, M_V)).reshape(B, S, M_WIDTH)
    hm = hm * jax.nn.sigmoid(mo)
    ym = hm @ w_bm

    lam_init = 0.8 - 0.6 * math.exp(-0.3 * layer)
    lq = lam_qk.astype(jnp.float32)
    lam = jnp.exp(jnp.sum(lq[0] * lq[1])) - jnp.exp(jnp.sum(lq[2] * lq[3])) + lam_init
    qa = partial_rotary(aq.reshape(B, S, A_HEADS, 2, A_QK), pos).transpose(0, 2, 3, 1, 4)
    ka = partial_rotary(ak.reshape(B, S, A_HEADS, 2, A_QK), pos).transpose(0, 2, 3, 1, 4)
    va = av.reshape(B, S, A_HEADS, A_V).transpose(0, 2, 1, 3)
    ha = diff_attention(qa, ka, va, lam).transpose(0, 2, 1, 3)
    ha = rms_norm(ha, a_norm.reshape(A_HEADS, A_V)) * (1.0 - lam_init)
    yd = ha.reshape(B, S, A_WIDTH) @ w_bd

    g_m, g_d = jnp.split(gt, N_BRANCH, axis=-1)
    y = jax.nn.sigmoid(g_m) * ym + jax.nn.sigmoid(g_d) * yd
    return y @ w_o


def conv_glu(h, w_up, conv_w, conv_b, w_down):
    S = h.shape[1]
    gate, val = jnp.split(h @ w_up, 2, axis=-1)
    gp = jnp.pad(gate, ((0, 0), (CONV_W - 1, 0), (0, 0)))
    conv = sum(conv_w[j] * gp[:, j:j + S] for j in range(CONV_W)) + conv_b
    return (jax.nn.silu(conv) * val) @ w_down


def setup_inputs(seed: int = 0) -> dict:
    key = jax.random.key(seed)
    ks = jax.random.split(key, 16)
    f32 = jnp.float32
    out_scale = (2.0 * DEPTH) ** -0.5

    def nrm(k, shape, scale):
        return jax.random.normal(k, shape, f32) * scale

    x = nrm(ks[0], (BATCH, SEQ, D_MODEL), 1.0)
    norm_mix = 1.0 + nrm(ks[1], (DEPTH, D_MODEL), 0.02)
    w_in = nrm(ks[2], (DEPTH, D_MODEL, N_IN), D_MODEL ** -0.5)
    b_in = nrm(ks[3], (DEPTH, N_IN), 0.01)
    b_in = b_in.at[:, F_GATE_OFF:F_GATE_OFF + M_HEADS].add(jnp.linspace(3.0, 6.0, M_HEADS, dtype=f32))
    m_norm = 1.0 + nrm(ks[4], (DEPTH, M_WIDTH), 0.02)
    a_norm = 1.0 + nrm(ks[5], (DEPTH, A_WIDTH), 0.02)
    lam_qk = nrm(ks[6], (DEPTH, 4, A_QK), 0.1)
    w_bm = nrm(ks[7], (DEPTH, M_WIDTH, D_MODEL), M_WIDTH ** -0.5)
    w_bd = nrm(ks[8], (DEPTH, A_WIDTH, D_MODEL), A_WIDTH ** -0.5)
    w_o = nrm(ks[9], (DEPTH, D_MODEL, D_MODEL), D_MODEL ** -0.5 * out_scale)
    norm_ffn = 1.0 + nrm(ks[10], (DEPTH, D_MODEL), 0.02)
    w_up = nrm(ks[11], (DEPTH, D_MODEL, 2 * D_FF), D_MODEL ** -0.5)
    conv_w = nrm(ks[12], (DEPTH, CONV_W, D_FF), CONV_W ** -0.5)
    conv_b = nrm(ks[13], (DEPTH, D_FF), 0.01)
    w_down = nrm(ks[14], (DEPTH, D_FF, D_MODEL), D_FF ** -0.5 * out_scale)
    norm_final = 1.0 + nrm(ks[15], (D_MODEL,), 0.02)
    return {"x": x, "norm_mix": norm_mix, "w_in": w_in, "b_in": b_in, "m_norm": m_norm,
            "a_norm": a_norm, "lam_qk": lam_qk, "w_bm": w_bm, "w_bd": w_bd, "w_o": w_o,
            "norm_ffn": norm_ffn, "w_up": w_up, "conv_w": conv_w, "conv_b": conv_b,
            "w_down": w_down, "norm_final": norm_final}


def reference(x, norm_mix, w_in, b_in, m_norm, a_norm, lam_qk, w_bm, w_bd, w_o,
              norm_ffn, w_up, conv_w, conv_b, w_down, norm_final):
    pos = jnp.arange(x.shape[1])
    h = x
    for l in range(DEPTH):
        h = h + hybrid_mixer(rms_norm(h, norm_mix[l]), pos, l, w_in[l], b_in[l], m_norm[l],
                             a_norm[l], lam_qk[l], w_bm[l], w_bd[l], w_o[l])
        h = h + conv_glu(rms_norm(h, norm_ffn[l]), w_up[l], conv_w[l], conv_b[l], w_down[l])
    return rms_norm(h, norm_final)
```

```python
import functools
import math

import jax
import jax.numpy as jnp
import numpy as np
from jax import lax
from jax.experimental import pallas as pl
from jax.experimental.pallas import tpu as pltpu

F32 = jnp.float32
BF16 = jnp.bfloat16

D_MODEL = 2048
BATCH = 4
SEQ = 2048
DEPTH = 4
TOKENS = BATCH * SEQ

M_HEADS = 8
M_QK = 128
M_V = 256
A_HEADS = 8
A_QK = 128
A_V = 256
ROPE_DIM = A_QK // 4
ROPE_THETA = 500000.0
D_FF = 5632
CONV_W = 3
EPS = 1e-6

_SPLITS = [M_HEADS * M_QK, M_HEADS * M_QK, M_HEADS * M_V, M_HEADS * M_V, M_HEADS, M_HEADS,
           2 * A_HEADS * A_QK, 2 * A_HEADS * A_QK, A_HEADS * A_V, 2 * D_MODEL]
_OFFS = [0] + [int(o) for o in np.cumsum(_SPLITS)]
GATE_LO, GATE_HI = _OFFS[4], _OFFS[6]
Z_Q, Z_K, Z_V, Z_O = 0, 1024, 2048, 4096
Z_AQ, Z_AK, Z_AV, Z_GT = 6144, 8192, 10240, 12288
N_Z = 16384
LANES = 128

VMEM_LIMIT = 56 * 1024 * 1024

M_CHUNK = 128
A_BLOCK = 256


def _cparams(sem):
    return pltpu.CompilerParams(dimension_semantics=sem, vmem_limit_bytes=VMEM_LIMIT)


def _rmsnorm_rows(x_ref, g_ref, out_ref, rows, chunk=256):
    def body(c, carry):
        r = pl.multiple_of(c * chunk, chunk)
        x = x_ref[pl.ds(r, chunk), :]
        ms = jnp.mean(x * x, axis=-1, keepdims=True)
        out_ref[pl.ds(r, chunk), :] = ((x * lax.rsqrt(ms + EPS)) * g_ref[...]).astype(out_ref.dtype)
        return carry
    lax.fori_loop(0, rows // chunk, body, 0)


def _inproj_kernel(x_ref, g_ref, w_ref, b_ref, wg_ref, bg_ref, z_ref, zg_ref, xn_ref, *, tm):
    @pl.when(pl.program_id(1) == 0)
    def _():
        _rmsnorm_rows(x_ref, g_ref, xn_ref, tm)
        zg_ref[...] = jnp.dot(xn_ref[...], wg_ref[...], preferred_element_type=F32) + bg_ref[...]

    z_ref[...] = (jnp.dot(xn_ref[...], w_ref[...], preferred_element_type=F32)
                  + b_ref[...]).astype(z_ref.dtype)


def _inproj(h, g, w_all, b_all, wg_all, bg_all, layer, tm=1024, tn=1024):
    return pl.pallas_call(
        functools.partial(_inproj_kernel, tm=tm),
        out_shape=(jax.ShapeDtypeStruct((TOKENS, N_Z), BF16),
                   jax.ShapeDtypeStruct((TOKENS, LANES), F32)),
        grid=(TOKENS // tm, N_Z // tn),
        in_specs=[
            pl.BlockSpec((tm, D_MODEL), lambda i, j: (i, 0)),
            pl.BlockSpec((None, 1, D_MODEL), lambda i, j: (layer, 0, 0)),
            pl.BlockSpec((None, D_MODEL, tn), lambda i, j: (layer, 0, j)),
            pl.BlockSpec((None, 1, tn), lambda i, j: (layer, 0, j)),
            pl.BlockSpec((None, D_MODEL, LANES), lambda i, j: (layer, 0, 0)),
            pl.BlockSpec((None, 1, LANES), lambda i, j: (layer, 0, 0)),
        ],
        out_specs=(pl.BlockSpec((tm, tn), lambda i, j: (i, j)),
                   pl.BlockSpec((tm, LANES), lambda i, j: (i, 0))),
        scratch_shapes=[pltpu.VMEM((tm, D_MODEL), BF16)],
        compiler_params=_cparams(("parallel", "arbitrary")),
        name="inproj",
    )(h, g, w_all, b_all, wg_all, bg_all)


def _log_sigmoid(x):
    return -(jnp.maximum(-x, 0.0) + jnp.log1p(jnp.exp(-jnp.abs(x))))


def _mlstm_kernel(q_ref, k_ref, v_ref, o_ref, zg_ref, mn_ref, out_ref, ct_ref, m_ref, *, L):
    @pl.when(pl.program_id(1) == 0)
    def _():
        ct_ref[...] = jnp.zeros_like(ct_ref)
        m_ref[...] = jnp.zeros_like(m_ref)

    scale = M_QK ** -0.5
    zg = zg_ref[...]
    zg_t = zg.T
    lf_cols = _log_sigmoid(zg)
    lf_rows = _log_sigmoid(zg_t[M_HEADS:2 * M_HEADS, :])
    i_rows = zg_t[0:M_HEADS, :]
    row = lax.broadcasted_iota(jnp.int32, (L, L), 0)
    col = lax.broadcasted_iota(jnp.int32, (L, L), 1)
    causal = row >= col
    upper = row <= col

    for h in range(M_HEADS):
        q = q_ref[:, h * M_QK:(h + 1) * M_QK]
        k = k_ref[:, h * M_QK:(h + 1) * M_QK]
        v = v_ref[:, h * M_V:(h + 1) * M_V]
        i_c = zg[:, h:h + 1]
        lf_c = lf_cols[:, M_HEADS + h:M_HEADS + h + 1]
        i_r = i_rows[h:h + 1, :]
        lf_r = lf_rows[h:h + 1, :]
        b_c = jnp.sum(jnp.where(causal, lf_r, 0.0), axis=1, keepdims=True)
        b_r = jnp.sum(jnp.where(upper, lf_c, 0.0), axis=0, keepdims=True)
        m_prev = m_ref[h:h + 1, 0:1]

        log_d = jnp.where(causal, b_c - b_r + i_r, -jnp.inf)
        inter_log = b_c + m_prev
        m_t = jnp.maximum(inter_log, jnp.max(log_d, axis=1, keepdims=True))
        d_mat = jnp.exp(log_d - m_t)
        inter = jnp.exp(inter_log - m_t)
        qk = lax.dot_general(q, k, (((1,), (1,)), ((), ())), preferred_element_type=F32)
        s_mat = (qk * scale) * d_mat
        ct = ct_ref[h]
        qc = jnp.dot(q, ct.astype(BF16), preferred_element_type=F32) * scale
        num = jnp.dot(s_mat.astype(BF16), v, preferred_element_type=F32) + inter * qc[:, :M_V]
        den = jnp.sum(s_mat, axis=1, keepdims=True) + inter * qc[:, M_V:M_V + 1]
        hh = num / jnp.maximum(jnp.abs(den), jnp.exp(-m_t))
        ms = jnp.mean(hh * hh, axis=-1, keepdims=True)
        hn = (hh * lax.rsqrt(ms + EPS)) * mn_ref[:, h * M_V:(h + 1) * M_V]
        og = o_ref[:, h * M_V:(h + 1) * M_V].astype(F32)
        out_ref[:, h * M_V:(h + 1) * M_V] = (hn * jax.nn.sigmoid(og)).astype(out_ref.dtype)

        g = b_c[L - 1:L, :]
        w_log = g - b_c + i_c
        m_new = jnp.maximum(g + m_prev, jnp.max(w_log, axis=0, keepdims=True))
        decay = jnp.exp(g + m_prev - m_new)
        w = jnp.exp(w_log - m_new)
        wv = jnp.concatenate([w * v.astype(F32), jnp.broadcast_to(w, (L, LANES))], axis=1)
        upd = lax.dot_general(k, wv.astype(BF16), (((0,), (0,)), ((), ())),
                              preferred_element_type=F32)
        ct_ref[h] = decay * ct + upd
        m_ref[h:h + 1, :] = jnp.broadcast_to(m_new, (1, LANES))


def _mlstm(z, zg, m_norm_all, layer, L=M_CHUNK):
    nc = SEQ // L
    row = lambda b, c: b * nc + c
    return pl.pallas_call(
        functools.partial(_mlstm_kernel, L=L),
        out_shape=jax.ShapeDtypeStruct((TOKENS, M_HEADS * M_V), BF16),
        grid=(BATCH, nc),
        in_specs=[
            pl.BlockSpec((L, 1024), lambda b, c: (row(b, c), Z_Q // 1024)),
            pl.BlockSpec((L, 1024), lambda b, c: (row(b, c), Z_K // 1024)),
            pl.BlockSpec((L, 2048), lambda b, c: (row(b, c), Z_V // 2048)),
            pl.BlockSpec((L, 2048), lambda b, c: (row(b, c), Z_O // 2048)),
            pl.BlockSpec((L, LANES), lambda b, c: (row(b, c), 0)),
            pl.BlockSpec((None, 1, M_HEADS * M_V), lambda b, c: (layer, 0, 0)),
        ],
        out_specs=pl.BlockSpec((L, M_HEADS * M_V), lambda b, c: (row(b, c), 0)),
        scratch_shapes=[pltpu.VMEM((M_HEADS, M_QK, M_V + LANES), F32),
                        pltpu.VMEM((M_HEADS, LANES), F32)],
        compiler_params=_cparams(("parallel", "arbitrary")),
        name="mlstm",
    )(z, z, z, z, zg, m_norm_all)


def _rope(x, cos, sin):
    half = ROPE_DIM // 2
    lane = lax.broadcasted_iota(jnp.int32, x.shape, 1)
    partner = jnp.where(lane < half, pltpu.roll(x, LANES - half, 1), pltpu.roll(x, half, 1))
    return x * cos + partner * sin


def _attn_kernel(q_ref, k_ref, v_ref, cos_ref, sin_ref, lq_ref, li_ref, an_ref, out_ref,
                 kr_ref, m_ref, l_ref, acc_ref, *, tq):
    qi = pl.program_id(2)
    scale = A_QK ** -0.5
    kchunk = 256

    @pl.when(qi == 0)
    def _():
        def body(c, carry):
            r = pl.multiple_of(c * kchunk, kchunk)
            kk = k_ref[pl.ds(r, kchunk), :].astype(F32)
            cos = cos_ref[pl.ds(r, kchunk), :]
            sin = sin_ref[pl.ds(r, kchunk), :]
            for mp in range(2):
                kr_ref[pl.ds(r, kchunk), mp * A_QK:(mp + 1) * A_QK] = _rope(
                    kk[:, mp * A_QK:(mp + 1) * A_QK], cos, sin).astype(BF16)
            return carry
        lax.fori_loop(0, SEQ // kchunk, body, 0)

    q0 = pl.multiple_of(qi * tq, tq)
    qf = q_ref[...].astype(F32)
    cos_q = cos_ref[pl.ds(q0, tq), :]
    sin_q = sin_ref[pl.ds(q0, tq), :]
    qr = [_rope(qf[:, mp * A_QK:(mp + 1) * A_QK], cos_q, sin_q).astype(BF16) for mp in range(2)]

    m_ref[...] = jnp.full_like(m_ref, -jnp.inf)
    l_ref[...] = jnp.zeros_like(l_ref)
    acc_ref[...] = jnp.zeros_like(acc_ref)

    def chunk(j, masked):
        r = pl.multiple_of(j * tq, tq)
        kj = kr_ref[pl.ds(r, tq), :]
        vj = v_ref[pl.ds(r, tq), :]
        for mp in range(2):
            s = lax.dot_general(qr[mp], kj[:, mp * A_QK:(mp + 1) * A_QK], (((1,), (1,)), ((), ())),
                                preferred_element_type=F32) * scale
            if masked:
                rr = lax.broadcasted_iota(jnp.int32, (tq, tq), 0)
                cc = lax.broadcasted_iota(jnp.int32, (tq, tq), 1)
                s = jnp.where(rr >= cc, s, -jnp.inf)
            m_old = m_ref[mp]
            m_new = jnp.maximum(m_old, jnp.max(s, axis=1, keepdims=True))
            alpha = jnp.exp(m_old - m_new)
            p = jnp.exp(s - m_new[:, 0:1])
            l_ref[mp] = alpha * l_ref[mp] + jnp.sum(p, axis=1, keepdims=True)
            acc_ref[mp] = alpha[:, 0:1] * acc_ref[mp] + jnp.dot(
                p.astype(BF16), vj, preferred_element_type=F32)
            m_ref[mp] = m_new

    def body(j, carry):
        chunk(j, False)
        return carry
    lax.fori_loop(0, qi, body, 0)
    chunk(qi, True)

    lq = lq_ref[...]
    lam_init = li_ref[0:1, 0:1]
    lam = (jnp.exp(jnp.sum(lq[0:1] * lq[1:2], axis=1, keepdims=True))
           - jnp.exp(jnp.sum(lq[2:3] * lq[3:4], axis=1, keepdims=True)) + lam_init)
    o = acc_ref[0] / l_ref[0][:, 0:1] - lam * (acc_ref[1] / l_ref[1][:, 0:1])
    ms = jnp.mean(o * o, axis=-1, keepdims=True)
    on = ((o * lax.rsqrt(ms + EPS)) * an_ref[...]) * (1.0 - lam_init)
    out_ref[...] = on.astype(out_ref.dtype)


def _attn(z, cos_t, sin_t, lam_qk, lam_init_arr, a_norm_all, layer, tq=A_BLOCK):
    nq = SEQ // tq
    return pl.pallas_call(
        functools.partial(_attn_kernel, tq=tq),
        out_shape=jax.ShapeDtypeStruct((TOKENS, A_HEADS * A_V), BF16),
        grid=(BATCH, A_HEADS, nq),
        in_specs=[
            pl.BlockSpec((tq, 2 * A_QK), lambda b, h, i: (b * nq + i, Z_AQ // 256 + h)),
            pl.BlockSpec((SEQ, 2 * A_QK), lambda b, h, i: (b, Z_AK // 256 + h)),
            pl.BlockSpec((SEQ, A_V), lambda b, h, i: (b, Z_AV // 256 + h)),
            pl.BlockSpec((SEQ, LANES), lambda b, h, i: (0, 0)),
            pl.BlockSpec((SEQ, LANES), lambda b, h, i: (0, 0)),
            pl.BlockSpec((None, 4, A_QK), lambda b, h, i: (layer, 0, 0)),
            pl.BlockSpec((None, 1, LANES), lambda b, h, i: (layer, 0, 0)),
            pl.BlockSpec((None, 1, A_V), lambda b, h, i: (layer, 0, h)),
        ],
        out_specs=pl.BlockSpec((tq, A_V), lambda b, h, i: (b * nq + i, h)),
        scratch_shapes=[pltpu.VMEM((SEQ, 2 * A_QK), BF16),
                        pltpu.VMEM((2, tq, LANES), F32),
                        pltpu.VMEM((2, tq, LANES), F32),
                        pltpu.VMEM((2, tq, A_V), F32)],
        compiler_params=_cparams(("parallel", "parallel", "arbitrary")),
        name="diff_attn",
    )(z, z, z, cos_t, sin_t, lam_qk, lam_init_arr, a_norm_all)


def _merge_kernel(hm_ref, ha_ref, wm_ref, wd_ref, gm_ref, gd_ref, y_ref):
    ym = jnp.dot(hm_ref[...], wm_ref[...], preferred_element_type=F32)
    yd = jnp.dot(ha_ref[...], wd_ref[...], preferred_element_type=F32)
    y = (jax.nn.sigmoid(gm_ref[...].astype(F32)) * ym + jax.nn.sigmoid(gd_ref[...].astype(F32)) * yd)
    y_ref[...] = y.astype(y_ref.dtype)


def _merge(hm, ha, wbm_all, wbd_all, z, layer, tm=1024, tn=512):
    return pl.pallas_call(
        _merge_kernel,
        out_shape=jax.ShapeDtypeStruct((TOKENS, D_MODEL), BF16),
        grid=(TOKENS // tm, D_MODEL // tn),
        in_specs=[
            pl.BlockSpec((tm, D_MODEL), lambda i, j: (i, 0)),
            pl.BlockSpec((tm, D_MODEL), lambda i, j: (i, 0)),
            pl.BlockSpec((None, D_MODEL, tn), lambda i, j: (layer, 0, j)),
            pl.BlockSpec((None, D_MODEL, tn), lambda i, j: (layer, 0, j)),
            pl.BlockSpec((tm, tn), lambda i, j: (i, Z_GT // tn + j)),
            pl.BlockSpec((tm, tn), lambda i, j: (i, (Z_GT + D_MODEL) // tn + j)),
        ],
        out_specs=pl.BlockSpec((tm, tn), lambda i, j: (i, j)),
        compiler_params=_cparams(("parallel", "arbitrary")),
        name="merge",
    )(hm, ha, wbm_all, wbd_all, z, z)


def _resproj_kernel(h_ref, y_ref, w_ref, out_ref):
    out_ref[...] = h_ref[...] + jnp.dot(y_ref[...], w_ref[...], preferred_element_type=F32)


def _resproj(h, y, w_all, layer, tm, tn, name):
    kdim = y.shape[1]
    return pl.pallas_call(
        _resproj_kernel,
        out_shape=jax.ShapeDtypeStruct((TOKENS, D_MODEL), F32),
        grid=(TOKENS // tm, D_MODEL // tn),
        in_specs=[
            pl.BlockSpec((tm, tn), lambda i, j: (i, j)),
            pl.BlockSpec((tm, kdim), lambda i, j: (i, 0)),
            pl.BlockSpec((None, kdim, tn), lambda i, j: (layer, 0, j)),
        ],
        out_specs=pl.BlockSpec((tm, tn), lambda i, j: (i, j)),
        compiler_params=_cparams(("parallel", "arbitrary")),
        name=name,
    )(h, y, w_all)


def _ffn_up_kernel(x_ref, g_ref, wg_ref, wv_ref, cw_ref, cb_ref, u_ref, xn_ref, carry_ref, *, tm, tn):
    i = pl.program_id(0)
    j = pl.program_id(1)

    @pl.when(j == 0)
    def _():
        _rmsnorm_rows(x_ref, g_ref, xn_ref, tm)

    xn = xn_ref[...]
    gate = jnp.dot(xn, wg_ref[...], preferred_element_type=F32)
    val = jnp.dot(xn, wv_ref[...], preferred_element_type=F32)

    @pl.when((i % (SEQ // tm)) == 0)
    def _():
        carry_ref[j] = jnp.zeros((8, tn), F32)

    prev = carry_ref[j]
    carry_ref[j] = gate[tm - 8:tm, :]

    rowi = lax.broadcasted_iota(jnp.int32, (tm, tn), 0)
    g1 = jnp.where(rowi == 0, prev[7:8, :], pltpu.roll(gate, 1, 0))
    g2 = jnp.where(rowi == 0, prev[6:7, :],
                   jnp.where(rowi == 1, prev[7:8, :], pltpu.roll(gate, 2, 0)))
    cw = cw_ref[...]
    conv = cw[0:1, :] * g2 + cw[1:2, :] * g1 + cw[2:3, :] * gate + cb_ref[...]
    u_ref[...] = ((conv * jax.nn.sigmoid(conv)) * val).astype(u_ref.dtype)


def _ffn_up(h, g_all, wup_all, cw_all, cb_all, layer, tm=1024, tn=512):
    nf = D_FF // tn
    return pl.pallas_call(
        functools.partial(_ffn_up_kernel, tm=tm, tn=tn),
        out_shape=jax.ShapeDtypeStruct((TOKENS, D_FF), BF16),
        grid=(TOKENS // tm, nf),
        in_specs=[
            pl.BlockSpec((tm, D_MODEL), lambda i, j: (i, 0)),
            pl.BlockSpec((None, 1, D_MODEL), lambda i, j: (layer, 0, 0)),
            pl.BlockSpec((None, D_MODEL, tn), lambda i, j: (layer, 0, j)),
            pl.BlockSpec((None, D_MODEL, tn), lambda i, j: (layer, 0, nf + j)),
            pl.BlockSpec((None, CONV_W, tn), lambda i, j: (layer, 0, j)),
            pl.BlockSpec((None, 1, tn), lambda i, j: (layer, 0, j)),
        ],
        out_specs=pl.BlockSpec((tm, tn), lambda i, j: (i, j)),
        scratch_shapes=[pltpu.VMEM((tm, D_MODEL), BF16),
                        pltpu.VMEM((nf, 8, tn), F32)],
        compiler_params=_cparams(("arbitrary", "arbitrary")),
        name="ffn_up",
    )(h, g_all, wup_all, wup_all, cw_all, cb_all)


def _final_norm_kernel(x_ref, g_ref, out_ref, *, tm):
    _rmsnorm_rows(x_ref, g_ref, out_ref, tm)


def _final_norm(h, g, tm=512):
    return pl.pallas_call(
        functools.partial(_final_norm_kernel, tm=tm),
        out_shape=jax.ShapeDtypeStruct((TOKENS, D_MODEL), F32),
        grid=(TOKENS // tm,),
        in_specs=[pl.BlockSpec((tm, D_MODEL), lambda i: (i, 0)),
                  pl.BlockSpec((1, D_MODEL), lambda i: (0, 0))],
        out_specs=pl.BlockSpec((tm, D_MODEL), lambda i: (i, 0)),
        compiler_params=_cparams(("parallel",)),
        name="final_norm",
    )(h, g)


def _rope_tables():
    half = ROPE_DIM // 2
    inv = ROPE_THETA ** (-jnp.arange(half, dtype=F32) / half)
    ang = jnp.arange(SEQ).astype(F32)[:, None] * inv[None, :]
    cos, sin = jnp.cos(ang), jnp.sin(ang)
    ones = jnp.ones((SEQ, LANES - ROPE_DIM), F32)
    zeros = jnp.zeros((SEQ, LANES - ROPE_DIM), F32)
    return (jnp.concatenate([cos, cos, ones], axis=1),
            jnp.concatenate([-sin, sin, zeros], axis=1))


def kernel(x, norm_mix, w_in, b_in, m_norm, a_norm, lam_qk, w_bm, w_bd, w_o, norm_ffn, w_up, conv_w, conv_b, w_down, norm_final):
    w_main = jnp.concatenate([w_in[:, :, :GATE_LO], w_in[:, :, GATE_HI:]], axis=2).astype(BF16)
    b_main = jnp.concatenate([b_in[:, :GATE_LO], b_in[:, GATE_HI:]], axis=1)[:, None, :]
    pad = LANES - (GATE_HI - GATE_LO)
    w_gate = jnp.pad(w_in[:, :, GATE_LO:GATE_HI], ((0, 0), (0, 0), (0, pad))).astype(BF16)
    b_gate = jnp.pad(b_in[:, GATE_LO:GATE_HI], ((0, 0), (0, pad)))[:, None, :]
    w_bm_b, w_bd_b, w_o_b = w_bm.astype(BF16), w_bd.astype(BF16), w_o.astype(BF16)
    w_up_b, w_down_b = w_up.astype(BF16), w_down.astype(BF16)
    norm_mix3, norm_ffn3 = norm_mix[:, None, :], norm_ffn[:, None, :]
    m_norm3, a_norm3 = m_norm[:, None, :], a_norm[:, None, :]
    conv_b3 = conv_b[:, None, :]
    lam_init = jnp.asarray([0.8 - 0.6 * math.exp(-0.3 * l) for l in range(DEPTH)], F32)
    lam_init3 = jnp.broadcast_to(lam_init[:, None, None], (DEPTH, 1, LANES))
    cos_t, sin_t = _rope_tables()

    h = x.reshape(TOKENS, D_MODEL)
    for l in range(DEPTH):
        z, zg = _inproj(h, norm_mix3, w_main, b_main, w_gate, b_gate, l)
        hm = _mlstm(z, zg, m_norm3, l)
        ha = _attn(z, cos_t, sin_t, lam_qk, lam_init3, a_norm3, l)
        y = _merge(hm, ha, w_bm_b, w_bd_b, z, l)
        h = _resproj(h, y, w_o_b, l, tm=1024, tn=1024, name="out_proj")
        u = _ffn_up(h, norm_ffn3, w_up_b, conv_w, conv_b3, l)
        h = _resproj(h, u, w_down_b, l, tm=512, tn=1024, name="ffn_down")
    out = _final_norm(h, norm_final[None, :])
    return out.reshape(BATCH, SEQ, D_MODEL)
```

```python
import functools
import math

import jax
import jax.numpy as jnp
import numpy as np
from jax import lax
from jax.experimental import pallas as pl
from jax.experimental.pallas import tpu as pltpu

F32 = jnp.float32
BF16 = jnp.bfloat16

D_MODEL = 2048
BATCH = 4
SEQ = 2048
DEPTH = 4
TOKENS = BATCH * SEQ

M_HEADS = 8
M_QK = 128
M_V = 256
A_HEADS = 8
A_QK = 128
A_V = 256
ROPE_DIM = A_QK // 4
ROPE_THETA = 500000.0
D_FF = 5632
CONV_W = 3
EPS = 1e-6

_SPLITS = [M_HEADS * M_QK, M_HEADS * M_QK, M_HEADS * M_V, M_HEADS * M_V, M_HEADS, M_HEADS,
           2 * A_HEADS * A_QK, 2 * A_HEADS * A_QK, A_HEADS * A_V, 2 * D_MODEL]
_OFFS = [0] + [int(o) for o in np.cumsum(_SPLITS)]
GATE_LO, GATE_HI = _OFFS[4], _OFFS[6]
Z_Q, Z_K, Z_V, Z_O = 0, 1024, 2048, 4096
Z_AQ, Z_AK, Z_AV, Z_GT = 6144, 8192, 10240, 12288
N_Z = 16384
LANES = 128

VMEM_LIMIT = 56 * 1024 * 1024

M_CHUNK = 256
A_BLOCK = 512


def _cparams(sem):
    return pltpu.CompilerParams(dimension_semantics=sem, vmem_limit_bytes=VMEM_LIMIT)


def _rmsnorm_rows(x_ref, g_ref, out_ref, rows, chunk=256):
    def body(c, carry):
        r = pl.multiple_of(c * chunk, chunk)
        x = x_ref[pl.ds(r, chunk), :]
        ms = jnp.mean(x * x, axis=-1, keepdims=True)
        out_ref[pl.ds(r, chunk), :] = ((x * lax.rsqrt(ms + EPS)) * g_ref[...]).astype(out_ref.dtype)
        return carry
    lax.fori_loop(0, rows // chunk, body, 0)


def _inproj_kernel(x_ref, g_ref, w_ref, b_ref, wg_ref, bg_ref, z_ref, zg_ref, xn_ref, *, tm):
    @pl.when(pl.program_id(1) == 0)
    def _():
        _rmsnorm_rows(x_ref, g_ref, xn_ref, tm)
        zg_ref[...] = jnp.dot(xn_ref[...], wg_ref[...], preferred_element_type=F32) + bg_ref[...]

    z_ref[...] = (jnp.dot(xn_ref[...], w_ref[...], preferred_element_type=F32)
                  + b_ref[...]).astype(z_ref.dtype)


def _inproj(h, g, w_all, b_all, wg_all, bg_all, layer, tm=1024, tn=1024):
    return pl.pallas_call(
        functools.partial(_inproj_kernel, tm=tm),
        out_shape=(jax.ShapeDtypeStruct((TOKENS, N_Z), BF16),
                   jax.ShapeDtypeStruct((TOKENS, LANES), F32)),
        grid=(TOKENS // tm, N_Z // tn),
        in_specs=[
            pl.BlockSpec((tm, D_MODEL), lambda i, j: (i, 0)),
            pl.BlockSpec((None, 1, D_MODEL), lambda i, j: (layer, 0, 0)),
            pl.BlockSpec((None, D_MODEL, tn), lambda i, j: (layer, 0, j)),
            pl.BlockSpec((None, 1, tn), lambda i, j: (layer, 0, j)),
            pl.BlockSpec((None, D_MODEL, LANES), lambda i, j: (layer, 0, 0)),
            pl.BlockSpec((None, 1, LANES), lambda i, j: (layer, 0, 0)),
        ],
        out_specs=(pl.BlockSpec((tm, tn), lambda i, j: (i, j)),
                   pl.BlockSpec((tm, LANES), lambda i, j: (i, 0))),
        scratch_shapes=[pltpu.VMEM((tm, D_MODEL), BF16)],
        compiler_params=_cparams(("parallel", "arbitrary")),
        name="inproj",
    )(h, g, w_all, b_all, wg_all, bg_all)


def _log_sigmoid(x):
    return -(jnp.maximum(-x, 0.0) + jnp.log1p(jnp.exp(-jnp.abs(x))))


def _mlstm_kernel(q_ref, k_ref, v_ref, o_ref, zg_ref, mn_ref, out_ref, ct_ref, m_ref, *, L):
    @pl.when(pl.program_id(1) == 0)
    def _():
        ct_ref[...] = jnp.zeros_like(ct_ref)
        m_ref[...] = jnp.zeros_like(m_ref)

    scale = M_QK ** -0.5
    zg = zg_ref[...]
    zg_t = zg.T
    lf_cols = _log_sigmoid(zg)
    lf_rows = _log_sigmoid(zg_t[M_HEADS:2 * M_HEADS, :])
    i_rows = zg_t[0:M_HEADS, :]
    row = lax.broadcasted_iota(jnp.int32, (L, L), 0)
    col = lax.broadcasted_iota(jnp.int32, (L, L), 1)
    causal = row >= col
    upper = row <= col

    for h in range(M_HEADS):
        q = q_ref[:, h * M_QK:(h + 1) * M_QK]
        k = k_ref[:, h * M_QK:(h + 1) * M_QK]
        v = v_ref[:, h * M_V:(h + 1) * M_V]
        i_c = zg[:, h:h + 1]
        lf_c = lf_cols[:, M_HEADS + h:M_HEADS + h + 1]
        i_r = i_rows[h:h + 1, :]
        lf_r = lf_rows[h:h + 1, :]
        b_c = jnp.sum(jnp.where(causal, lf_r, 0.0), axis=1, keepdims=True)
        b_r = jnp.sum(jnp.where(upper, lf_c, 0.0), axis=0, keepdims=True)
        m_prev = m_ref[h:h + 1, 0:1]

        log_d = jnp.where(causal, b_c - b_r + i_r, -jnp.inf)
        inter_log = b_c + m_prev
        m_t = jnp.maximum(inter_log, jnp.max(log_d, axis=1, keepdims=True))
        d_mat = jnp.exp(log_d - m_t)
        inter = jnp.exp(inter_log - m_t)
        qk = lax.dot_general(q, k, (((1,), (1,)), ((), ())), preferred_element_type=F32)
        s_mat = (qk * scale) * d_mat
        ct = ct_ref[h]
        qc = jnp.dot(q, ct.astype(BF16), preferred_element_type=F32) * scale
        num = jnp.dot(s_mat.astype(BF16), v, preferred_element_type=F32) + inter * qc[:, :M_V]
        den = jnp.sum(s_mat, axis=1, keepdims=True) + inter * qc[:, M_V:M_V + 1]
        hh = num / jnp.maximum(jnp.abs(den), jnp.exp(-m_t))
        ms = jnp.mean(hh * hh, axis=-1, keepdims=True)
        hn = (hh * lax.rsqrt(ms + EPS)) * mn_ref[:, h * M_V:(h + 1) * M_V]
        og = o_ref[:, h * M_V:(h + 1) * M_V].astype(F32)
        out_ref[:, h * M_V:(h + 1) * M_V] = (hn * jax.nn.sigmoid(og)).astype(out_ref.dtype)

        g = b_c[L - 1:L, :]
        w_log = g - b_c + i_c
        m_new = jnp.maximum(g + m_prev, jnp.max(w_log, axis=0, keepdims=True))
        decay = jnp.exp(g + m_prev - m_new)
        w = jnp.exp(w_log - m_new)
        wv = jnp.concatenate([w * v.astype(F32), jnp.broadcast_to(w, (L, LANES))], axis=1)
        upd = lax.dot_general(k, wv.astype(BF16), (((0,), (0,)), ((), ())),
                              preferred_element_type=F32)
        ct_ref[h] = decay * ct + upd
        m_ref[h:h + 1, :] = jnp.broadcast_to(m_new, (1, LANES))


def _mlstm(z, zg, m_norm_all, layer, L=M_CHUNK):
    nc = SEQ // L
    row = lambda b, c: b * nc + c
    return pl.pallas_call(
        functools.partial(_mlstm_kernel, L=L),
        out_shape=jax.ShapeDtypeStruct((TOKENS, M_HEADS * M_V), BF16),
        grid=(BATCH, nc),
        in_specs=[
            pl.BlockSpec((L, 1024), lambda b, c: (row(b, c), Z_Q // 1024)),
            pl.BlockSpec((L, 1024), lambda b, c: (row(b, c), Z_K // 1024)),
            pl.BlockSpec((L, 2048), lambda b, c: (row(b, c), Z_V // 2048)),
            pl.BlockSpec((L, 2048), lambda b, c: (row(b, c), Z_O // 2048)),
            pl.BlockSpec((L, LANES), lambda b, c: (row(b, c), 0)),
            pl.BlockSpec((None, 1, M_HEADS * M_V), lambda b, c: (layer, 0, 0)),
        ],
        out_specs=pl.BlockSpec((L, M_HEADS * M_V), lambda b, c: (row(b, c), 0)),
        scratch_shapes=[pltpu.VMEM((M_HEADS, M_QK, M_V + LANES), F32),
                        pltpu.VMEM((M_HEADS, LANES), F32)],
        compiler_params=_cparams(("parallel", "arbitrary")),
        name="mlstm",
    )(z, z, z, z, zg, m_norm_all)


def _rope(x, cos, sin):
    half = ROPE_DIM // 2
    lane = lax.broadcasted_iota(jnp.int32, x.shape, 1)
    partner = jnp.where(lane < half, pltpu.roll(x, LANES - half, 1), pltpu.roll(x, half, 1))
    return x * cos + partner * sin


def _attn_kernel(q_ref, k_ref, v_ref, cos_ref, sin_ref, lq_ref, li_ref, an_ref, out_ref,
                 kr_ref, m_ref, l_ref, acc_ref, *, tq):
    qi = pl.program_id(2)
    scale = A_QK ** -0.5
    kchunk = 256

    @pl.when(qi == 0)
    def _():
        def body(c, carry):
            r = pl.multiple_of(c * kchunk, kchunk)
            kk = k_ref[pl.ds(r, kchunk), :].astype(F32)
            cos = cos_ref[pl.ds(r, kchunk), :]
            sin = sin_ref[pl.ds(r, kchunk), :]
            for mp in range(2):
                kr_ref[pl.ds(r, kchunk), mp * A_QK:(mp + 1) * A_QK] = _rope(
                    kk[:, mp * A_QK:(mp + 1) * A_QK], cos, sin).astype(BF16)
            return carry
        lax.fori_loop(0, SEQ // kchunk, body, 0)

    q0 = pl.multiple_of(qi * tq, tq)
    qf = q_ref[...].astype(F32)
    cos_q = cos_ref[pl.ds(q0, tq), :]
    sin_q = sin_ref[pl.ds(q0, tq), :]
    qscale = scale * math.log2(math.e)
    qr = [(_rope(qf[:, mp * A_QK:(mp + 1) * A_QK], cos_q, sin_q) * qscale).astype(BF16)
          for mp in range(2)]

    m_ref[...] = jnp.full_like(m_ref, -jnp.inf)
    l_ref[...] = jnp.zeros_like(l_ref)
    acc_ref[...] = jnp.zeros_like(acc_ref)
    ng = tq // LANES

    def lane_tile(a, n):
        return jnp.concatenate([a] * n, axis=1)

    def chunk(j, masked):
        r = pl.multiple_of(j * tq, tq)
        kj = kr_ref[pl.ds(r, tq), :]
        vj = v_ref[pl.ds(r, tq), :]
        for mp in range(2):
            s = lax.dot_general(qr[mp], kj[:, mp * A_QK:(mp + 1) * A_QK], (((1,), (1,)), ((), ())),
                                preferred_element_type=F32)
            if masked:
                rr = lax.broadcasted_iota(jnp.int32, (tq, tq), 0)
                cc = lax.broadcasted_iota(jnp.int32, (tq, tq), 1)
                s = jnp.where(rr >= cc, s, -jnp.inf)
            m_old = m_ref[mp]
            mc = s[:, 0:LANES]
            for g in range(1, ng):
                mc = jnp.maximum(mc, s[:, g * LANES:(g + 1) * LANES])
            m_new = jnp.maximum(m_old, jnp.max(mc, axis=1, keepdims=True))
            alpha = jnp.exp2(m_old - m_new)
            p = jnp.exp2(s - lane_tile(m_new, ng))
            ps = p[:, 0:LANES]
            for g in range(1, ng):
                ps = ps + p[:, g * LANES:(g + 1) * LANES]
            l_ref[mp] = alpha * l_ref[mp] + ps
            acc_ref[mp] = lane_tile(alpha, A_V // LANES) * acc_ref[mp] + jnp.dot(
                p.astype(BF16), vj, preferred_element_type=F32)
            m_ref[mp] = m_new

    def body(j, carry):
        chunk(j, False)
        return carry
    lax.fori_loop(0, qi, body, 0)
    chunk(qi, True)

    lq = lq_ref[...]
    lam_init = li_ref[0:1, 0:1]
    lam = (jnp.exp(jnp.sum(lq[0:1] * lq[1:2], axis=1, keepdims=True))
           - jnp.exp(jnp.sum(lq[2:3] * lq[3:4], axis=1, keepdims=True)) + lam_init)
    l1 = jnp.sum(l_ref[0], axis=1, keepdims=True)
    l2 = jnp.sum(l_ref[1], axis=1, keepdims=True)
    o = acc_ref[0] / l1 - lam * (acc_ref[1] / l2)
    ms = jnp.mean(o * o, axis=-1, keepdims=True)
    on = ((o * lax.rsqrt(ms + EPS)) * an_ref[...]) * (1.0 - lam_init)
    out_ref[...] = on.astype(out_ref.dtype)


def _attn(z, cos_t, sin_t, lam_qk, lam_init_arr, a_norm_all, layer, tq=A_BLOCK):
    nq = SEQ // tq
    return pl.pallas_call(
        functools.partial(_attn_kernel, tq=tq),
        out_shape=jax.ShapeDtypeStruct((TOKENS, A_HEADS * A_V), BF16),
        grid=(BATCH, A_HEADS, nq),
        in_specs=[
            pl.BlockSpec((tq, 2 * A_QK), lambda b, h, i: (b * nq + i, Z_AQ // 256 + h)),
            pl.BlockSpec((SEQ, 2 * A_QK), lambda b, h, i: (b, Z_AK // 256 + h)),
            pl.BlockSpec((SEQ, A_V), lambda b, h, i: (b, Z_AV // 256 + h)),
            pl.BlockSpec((SEQ, LANES), lambda b, h, i: (0, 0)),
            pl.BlockSpec((SEQ, LANES), lambda b, h, i: (0, 0)),
            pl.BlockSpec((None, 4, A_QK), lambda b, h, i: (layer, 0, 0)),
            pl.BlockSpec((None, 1, LANES), lambda b, h, i: (layer, 0, 0)),
            pl.BlockSpec((None, 1, A_V), lambda b, h, i: (layer, 0, h)),
        ],
        out_specs=pl.BlockSpec((tq, A_V), lambda b, h, i: (b * nq + i, h)),
        scratch_shapes=[pltpu.VMEM((SEQ, 2 * A_QK), BF16),
                        pltpu.VMEM((2, tq, LANES), F32),
                        pltpu.VMEM((2, tq, LANES), F32),
                        pltpu.VMEM((2, tq, A_V), F32)],
        compiler_params=_cparams(("parallel", "parallel", "arbitrary")),
        name="diff_attn",
    )(z, z, z, cos_t, sin_t, lam_qk, lam_init_arr, a_norm_all)


def _merge_kernel(hm_ref, ha_ref, wm_ref, wd_ref, gm_ref, gd_ref, y_ref):
    ym = jnp.dot(hm_ref[...], wm_ref[...], preferred_element_type=F32)
    yd = jnp.dot(ha_ref[...], wd_ref[...], preferred_element_type=F32)
    y = (jax.nn.sigmoid(gm_ref[...].astype(F32)) * ym + jax.nn.sigmoid(gd_ref[...].astype(F32)) * yd)
    y_ref[...] = y.astype(y_ref.dtype)


def _merge(hm, ha, wbm_all, wbd_all, z, layer, tm=1024, tn=512):
    return pl.pallas_call(
        _merge_kernel,
        out_shape=jax.ShapeDtypeStruct((TOKENS, D_MODEL), BF16),
        grid=(TOKENS // tm, D_MODEL // tn),
        in_specs=[
            pl.BlockSpec((tm, D_MODEL), lambda i, j: (i, 0)),
            pl.BlockSpec((tm, D_MODEL), lambda i, j: (i, 0)),
            pl.BlockSpec((None, D_MODEL, tn), lambda i, j: (layer, 0, j)),
            pl.BlockSpec((None, D_MODEL, tn), lambda i, j: (layer, 0, j)),
            pl.BlockSpec((tm, tn), lambda i, j: (i, Z_GT // tn + j)),
            pl.BlockSpec((tm, tn), lambda i, j: (i, (Z_GT + D_MODEL) // tn + j)),
        ],
        out_specs=pl.BlockSpec((tm, tn), lambda i, j: (i, j)),
        compiler_params=_cparams(("parallel", "arbitrary")),
        name="merge",
    )(hm, ha, wbm_all, wbd_all, z, z)


def _resproj_kernel(h_ref, y_ref, w_ref, out_ref):
    out_ref[...] = h_ref[...] + jnp.dot(y_ref[...], w_ref[...], preferred_element_type=F32)


def _resproj(h, y, w_all, layer, tm, tn, name):
    kdim = y.shape[1]
    return pl.pallas_call(
        _resproj_kernel,
        out_shape=jax.ShapeDtypeStruct((TOKENS, D_MODEL), F32),
        grid=(TOKENS // tm, D_MODEL // tn),
        in_specs=[
            pl.BlockSpec((tm, tn), lambda i, j: (i, j)),
            pl.BlockSpec((tm, kdim), lambda i, j: (i, 0)),
            pl.BlockSpec((None, kdim, tn), lambda i, j: (layer, 0, j)),
        ],
        out_specs=pl.BlockSpec((tm, tn), lambda i, j: (i, j)),
        compiler_params=_cparams(("parallel", "arbitrary")),
        name=name,
    )(h, y, w_all)


def _ffn_up_kernel(x_ref, g_ref, wg_ref, wv_ref, cw_ref, cb_ref, u_ref, xn_ref, carry_ref, *, tm, tn):
    i = pl.program_id(0)
    j = pl.program_id(1)

    @pl.when(j == 0)
    def _():
        _rmsnorm_rows(x_ref, g_ref, xn_ref, tm)

    xn = xn_ref[...]
    gate = jnp.dot(xn, wg_ref[...], preferred_element_type=F32)
    val = jnp.dot(xn, wv_ref[...], preferred_element_type=F32)

    @pl.when((i % (SEQ // tm)) == 0)
    def _():
        carry_ref[j] = jnp.zeros((8, tn), F32)

    prev = carry_ref[j]
    carry_ref[j] = gate[tm - 8:tm, :]

    rowi = lax.broadcasted_iota(jnp.int32, (tm, tn), 0)
    g1 = jnp.where(rowi == 0, prev[7:8, :], pltpu.roll(gate, 1, 0))
    g2 = jnp.where(rowi == 0, prev[6:7, :],
                   jnp.where(rowi == 1, prev[7:8, :], pltpu.roll(gate, 2, 0)))
    cw = cw_ref[...]
    conv = cw[0:1, :] * g2 + cw[1:2, :] * g1 + cw[2:3, :] * gate + cb_ref[...]
    u_ref[...] = ((conv * jax.nn.sigmoid(conv)) * val).astype(u_ref.dtype)


def _ffn_up(h, g_all, wup_all, cw_all, cb_all, layer, tm=1024, tn=512):
    nf = D_FF // tn
    return pl.pallas_call(
        functools.partial(_ffn_up_kernel, tm=tm, tn=tn),
        out_shape=jax.ShapeDtypeStruct((TOKENS, D_FF), BF16),
        grid=(TOKENS // tm, nf),
        in_specs=[
            pl.BlockSpec((tm, D_MODEL), lambda i, j: (i, 0)),
            pl.BlockSpec((None, 1, D_MODEL), lambda i, j: (layer, 0, 0)),
            pl.BlockSpec((None, D_MODEL, tn), lambda i, j: (layer, 0, j)),
            pl.BlockSpec((None, D_MODEL, tn), lambda i, j: (layer, 0, nf + j)),
            pl.BlockSpec((None, CONV_W, tn), lambda i, j: (layer, 0, j)),
            pl.BlockSpec((None, 1, tn), lambda i, j: (layer, 0, j)),
        ],
        out_specs=pl.BlockSpec((tm, tn), lambda i, j: (i, j)),
        scratch_shapes=[pltpu.VMEM((tm, D_MODEL), BF16),
                        pltpu.VMEM((nf, 8, tn), F32)],
        compiler_params=_cparams(("arbitrary", "arbitrary")),
        name="ffn_up",
    )(h, g_all, wup_all, wup_all, cw_all, cb_all)


def _final_norm_kernel(x_ref, g_ref, out_ref, *, tm):
    _rmsnorm_rows(x_ref, g_ref, out_ref, tm)


def _final_norm(h, g, tm=512):
    return pl.pallas_call(
        functools.partial(_final_norm_kernel, tm=tm),
        out_shape=jax.ShapeDtypeStruct((TOKENS, D_MODEL), F32),
        grid=(TOKENS // tm,),
        in_specs=[pl.BlockSpec((tm, D_MODEL), lambda i: (i, 0)),
                  pl.BlockSpec((1, D_MODEL), lambda i: (0, 0))],
        out_specs=pl.BlockSpec((tm, D_MODEL), lambda i: (i, 0)),
        compiler_params=_cparams(("parallel",)),
        name="final_norm",
    )(h, g)


def _rope_tables():
    half = ROPE_DIM // 2
    inv = ROPE_THETA ** (-jnp.arange(half, dtype=F32) / half)
    ang = jnp.arange(SEQ).astype(F32)[:, None] * inv[None, :]
    cos, sin = jnp.cos(ang), jnp.sin(ang)
    ones = jnp.ones((SEQ, LANES - ROPE_DIM), F32)
    zeros = jnp.zeros((SEQ, LANES - ROPE_DIM), F32)
    return (jnp.concatenate([cos, cos, ones], axis=1),
            jnp.concatenate([-sin, sin, zeros], axis=1))


def kernel(x, norm_mix, w_in, b_in, m_norm, a_norm, lam_qk, w_bm, w_bd, w_o, norm_ffn, w_up, conv_w, conv_b, w_down, norm_final):
    w_main = jnp.concatenate([w_in[:, :, :GATE_LO], w_in[:, :, GATE_HI:]], axis=2).astype(BF16)
    b_main = jnp.concatenate([b_in[:, :GATE_LO], b_in[:, GATE_HI:]], axis=1)[:, None, :]
    pad = LANES - (GATE_HI - GATE_LO)
    w_gate = jnp.pad(w_in[:, :, GATE_LO:GATE_HI], ((0, 0), (0, 0), (0, pad))).astype(BF16)
    b_gate = jnp.pad(b_in[:, GATE_LO:GATE_HI], ((0, 0), (0, pad)))[:, None, :]
    w_bm_b, w_bd_b, w_o_b = w_bm.astype(BF16), w_bd.astype(BF16), w_o.astype(BF16)
    w_up_b, w_down_b = w_up.astype(BF16), w_down.astype(BF16)
    norm_mix3, norm_ffn3 = norm_mix[:, None, :], norm_ffn[:, None, :]
    m_norm3, a_norm3 = m_norm[:, None, :], a_norm[:, None, :]
    conv_b3 = conv_b[:, None, :]
    lam_init = jnp.asarray([0.8 - 0.6 * math.exp(-0.3 * l) for l in range(DEPTH)], F32)
    lam_init3 = jnp.broadcast_to(lam_init[:, None, None], (DEPTH, 1, LANES))
    cos_t, sin_t = _rope_tables()

    h = x.reshape(TOKENS, D_MODEL)
    for l in range(DEPTH):
        z, zg = _inproj(h, norm_mix3, w_main, b_main, w_gate, b_gate, l)
        hm = _mlstm(z, zg, m_norm3, l)
        ha = _attn(z, cos_t, sin_t, lam_qk, lam_init3, a_norm3, l)
        y = _merge(hm, ha, w_bm_b, w_bd_b, z, l)
        h = _resproj(h, y, w_o_b, l, tm=1024, tn=1024, name="out_proj")
        u = _ffn_up(h, norm_ffn3, w_up_b, conv_w, conv_b3, l)
        h = _resproj(h, u, w_down_b, l, tm=512, tn=1024, name="ffn_down")
    out = _final_norm(h, norm_final[None, :])
    return out.reshape(BATCH, SEQ, D_MODEL)
```

```python
import functools
import math

import jax
import jax.numpy as jnp
import numpy as np
from jax import lax
from jax.experimental import pallas as pl
from jax.experimental.pallas import tpu as pltpu

F32 = jnp.float32
BF16 = jnp.bfloat16

D_MODEL = 2048
BATCH = 4
SEQ = 2048
DEPTH = 4
TOKENS = BATCH * SEQ

M_HEADS = 8
M_QK = 128
M_V = 256
A_HEADS = 8
A_QK = 128
A_V = 256
ROPE_DIM = A_QK // 4
ROPE_THETA = 500000.0
D_FF = 5632
CONV_W = 3
EPS = 1e-6

_SPLITS = [M_HEADS * M_QK, M_HEADS * M_QK, M_HEADS * M_V, M_HEADS * M_V, M_HEADS, M_HEADS,
           2 * A_HEADS * A_QK, 2 * A_HEADS * A_QK, A_HEADS * A_V, 2 * D_MODEL]
_OFFS = [0] + [int(o) for o in np.cumsum(_SPLITS)]
GATE_LO, GATE_HI = _OFFS[4], _OFFS[6]
Z_Q, Z_K, Z_V, Z_O = 0, 1024, 2048, 4096
Z_AQ, Z_AK, Z_AV, Z_GT = 6144, 8192, 10240, 12288
N_Z = 16384
LANES = 128

VMEM_LIMIT = 56 * 1024 * 1024

M_CHUNK = 256
A_BLOCK = 256


def _cparams(sem):
    return pltpu.CompilerParams(dimension_semantics=sem, vmem_limit_bytes=VMEM_LIMIT)


def _rmsnorm_rows(x_ref, g_ref, out_ref, rows, chunk=256):
    def body(c, carry):
        r = pl.multiple_of(c * chunk, chunk)
        x = x_ref[pl.ds(r, chunk), :]
        ms = jnp.mean(x * x, axis=-1, keepdims=True)
        out_ref[pl.ds(r, chunk), :] = ((x * lax.rsqrt(ms + EPS)) * g_ref[...]).astype(out_ref.dtype)
        return carry
    lax.fori_loop(0, rows // chunk, body, 0)


def _inproj_kernel(x_ref, g_ref, w_ref, b_ref, wg_ref, bg_ref, z_ref, zg_ref, xn_ref, *, tm):
    @pl.when(pl.program_id(1) == 0)
    def _():
        _rmsnorm_rows(x_ref, g_ref, xn_ref, tm)
        zg_ref[...] = jnp.dot(xn_ref[...], wg_ref[...], preferred_element_type=F32) + bg_ref[...]

    z_ref[...] = (jnp.dot(xn_ref[...], w_ref[...], preferred_element_type=F32)
                  + b_ref[...]).astype(z_ref.dtype)


def _inproj(h, g, w_all, b_all, wg_all, bg_all, layer, tm=1024, tn=1024):
    return pl.pallas_call(
        functools.partial(_inproj_kernel, tm=tm),
        out_shape=(jax.ShapeDtypeStruct((TOKENS, N_Z), BF16),
                   jax.ShapeDtypeStruct((TOKENS, LANES), F32)),
        grid=(TOKENS // tm, N_Z // tn),
        in_specs=[
            pl.BlockSpec((tm, D_MODEL), lambda i, j: (i, 0)),
            pl.BlockSpec((None, 1, D_MODEL), lambda i, j: (layer, 0, 0)),
            pl.BlockSpec((None, D_MODEL, tn), lambda i, j: (layer, 0, j)),
            pl.BlockSpec((None, 1, tn), lambda i, j: (layer, 0, j)),
            pl.BlockSpec((None, D_MODEL, LANES), lambda i, j: (layer, 0, 0)),
            pl.BlockSpec((None, 1, LANES), lambda i, j: (layer, 0, 0)),
        ],
        out_specs=(pl.BlockSpec((tm, tn), lambda i, j: (i, j)),
                   pl.BlockSpec((tm, LANES), lambda i, j: (i, 0))),
        scratch_shapes=[pltpu.VMEM((tm, D_MODEL), BF16)],
        compiler_params=_cparams(("parallel", "arbitrary")),
        name="inproj",
    )(h, g, w_all, b_all, wg_all, bg_all)


def _log_sigmoid(x):
    return -(jnp.maximum(-x, 0.0) + jnp.log1p(jnp.exp(-jnp.abs(x))))


def _mlstm_kernel(q_ref, k_ref, v_ref, o_ref, zg_ref, mn_ref, out_ref, ct_ref, m_ref, *, L):
    @pl.when(pl.program_id(1) == 0)
    def _():
        ct_ref[...] = jnp.zeros_like(ct_ref)
        m_ref[...] = jnp.zeros_like(m_ref)

    scale = M_QK ** -0.5
    zg = zg_ref[...]
    zg_t = zg.T
    lf_cols = _log_sigmoid(zg)
    lf_rows = _log_sigmoid(zg_t[M_HEADS:2 * M_HEADS, :])
    i_rows = zg_t[0:M_HEADS, :]
    row = lax.broadcasted_iota(jnp.int32, (L, L), 0)
    col = lax.broadcasted_iota(jnp.int32, (L, L), 1)
    causal = row >= col
    upper = row <= col

    for h in range(M_HEADS):
        q = q_ref[:, h * M_QK:(h + 1) * M_QK]
        k = k_ref[:, h * M_QK:(h + 1) * M_QK]
        v = v_ref[:, h * M_V:(h + 1) * M_V]
        i_c = zg[:, h:h + 1]
        lf_c = lf_cols[:, M_HEADS + h:M_HEADS + h + 1]
        i_r = i_rows[h:h + 1, :]
        lf_r = lf_rows[h:h + 1, :]
        b_c = jnp.sum(jnp.where(causal, lf_r, 0.0), axis=1, keepdims=True)
        b_r = jnp.sum(jnp.where(upper, lf_c, 0.0), axis=0, keepdims=True)
        m_prev = m_ref[h:h + 1, 0:1]

        log_d = jnp.where(causal, b_c - b_r + i_r, -jnp.inf)
        inter_log = b_c + m_prev
        m_t = jnp.maximum(inter_log, jnp.max(log_d, axis=1, keepdims=True))
        d_mat = jnp.exp(log_d - m_t)
        inter = jnp.exp(inter_log - m_t)
        qk = lax.dot_general(q, k, (((1,), (1,)), ((), ())), preferred_element_type=F32)
        s_mat = (qk * scale) * d_mat
        ct = ct_ref[h]
        qc = jnp.dot(q, ct.astype(BF16), preferred_element_type=F32) * scale
        num = jnp.dot(s_mat.astype(BF16), v, preferred_element_type=F32) + inter * qc[:, :M_V]
        den = jnp.sum(s_mat, axis=1, keepdims=True) + inter * qc[:, M_V:M_V + 1]
        hh = num / jnp.maximum(jnp.abs(den), jnp.exp(-m_t))
        ms = jnp.mean(hh * hh, axis=-1, keepdims=True)
        hn = (hh * lax.rsqrt(ms + EPS)) * mn_ref[:, h * M_V:(h + 1) * M_V]
        og = o_ref[:, h * M_V:(h + 1) * M_V].astype(F32)
        out_ref[:, h * M_V:(h + 1) * M_V] = (hn * jax.nn.sigmoid(og)).astype(out_ref.dtype)

        g = b_c[L - 1:L, :]
        w_log = g - b_c + i_c
        m_new = jnp.maximum(g + m_prev, jnp.max(w_log, axis=0, keepdims=True))
        decay = jnp.exp(g + m_prev - m_new)
        w = jnp.exp(w_log - m_new)
        wv = jnp.concatenate([w * v.astype(F32), jnp.broadcast_to(w, (L, LANES))], axis=1)
        upd = lax.dot_general(k, wv.astype(BF16), (((0,), (0,)), ((), ())),
                              preferred_element_type=F32)
        ct_ref[h] = decay * ct + upd
        m_ref[h:h + 1, :] = jnp.broadcast_to(m_new, (1, LANES))


def _mlstm(z, zg, m_norm_all, layer, L=M_CHUNK):
    nc = SEQ // L
    row = lambda b, c: b * nc + c
    return pl.pallas_call(
        functools.partial(_mlstm_kernel, L=L),
        out_shape=jax.ShapeDtypeStruct((TOKENS, M_HEADS * M_V), BF16),
        grid=(BATCH, nc),
        in_specs=[
            pl.BlockSpec((L, 1024), lambda b, c: (row(b, c), Z_Q // 1024)),
            pl.BlockSpec((L, 1024), lambda b, c: (row(b, c), Z_K // 1024)),
            pl.BlockSpec((L, 2048), lambda b, c: (row(b, c), Z_V // 2048)),
            pl.BlockSpec((L, 2048), lambda b, c: (row(b, c), Z_O // 2048)),
            pl.BlockSpec((L, LANES), lambda b, c: (row(b, c), 0)),
            pl.BlockSpec((None, 1, M_HEADS * M_V), lambda b, c: (layer, 0, 0)),
        ],
        out_specs=pl.BlockSpec((L, M_HEADS * M_V), lambda b, c: (row(b, c), 0)),
        scratch_shapes=[pltpu.VMEM((M_HEADS, M_QK, M_V + LANES), F32),
                        pltpu.VMEM((M_HEADS, LANES), F32)],
        compiler_params=_cparams(("parallel", "arbitrary")),
        name="mlstm",
    )(z, z, z, z, zg, m_norm_all)


def _rope(x, cos, sin):
    half = ROPE_DIM // 2
    lane = lax.broadcasted_iota(jnp.int32, x.shape, 1)
    partner = jnp.where(lane < half, pltpu.roll(x, LANES - half, 1), pltpu.roll(x, half, 1))
    return x * cos + partner * sin


def _lane_groups(x, op):
    acc = x[:, 0:LANES]
    for g in range(1, x.shape[1] // LANES):
        acc = op(acc, x[:, g * LANES:(g + 1) * LANES])
    return acc


def _lane_tile(a, width):
    return jnp.concatenate([a] * (width // LANES), axis=1) if width > LANES else a


def _attn_kernel(q_ref, k_ref, v_ref, cos_ref, sin_ref, lq_ref, li_ref, an_ref, out_ref, kr_ref, *, tq):
    qscale = (A_QK ** -0.5) * math.log2(math.e)
    nt = (((1,), (1,)), ((), ()))
    rchunk = 256
    for c in range(SEQ // rchunk):
        r = c * rchunk
        kk = k_ref[r:r + rchunk, :].astype(F32)
        cos = cos_ref[r:r + rchunk, :]
        sin = sin_ref[r:r + rchunk, :]
        for mp in range(2):
            kr_ref[r:r + rchunk, mp * A_QK:(mp + 1) * A_QK] = _rope(
                kk[:, mp * A_QK:(mp + 1) * A_QK], cos, sin).astype(BF16)

    lq = lq_ref[...]
    lam_init = li_ref[0:1, 0:1]
    lam = (jnp.exp(jnp.sum(lq[0:1] * lq[1:2], axis=1, keepdims=True))
           - jnp.exp(jnp.sum(lq[2:3] * lq[3:4], axis=1, keepdims=True)) + lam_init)
    causal = (lax.broadcasted_iota(jnp.int32, (tq, tq), 0)
              >= lax.broadcasted_iota(jnp.int32, (tq, tq), 1))

    for t in range(SEQ // tq):
        r0 = t * tq
        qf = q_ref[r0:r0 + tq, :].astype(F32)
        cos_q = cos_ref[r0:r0 + tq, :]
        sin_q = sin_ref[r0:r0 + tq, :]
        pd, pf, inv = [], [], []
        for mp in range(2):
            qr = (_rope(qf[:, mp * A_QK:(mp + 1) * A_QK], cos_q, sin_q) * qscale).astype(BF16)
            s_d = lax.dot_general(qr, kr_ref[r0:r0 + tq, mp * A_QK:(mp + 1) * A_QK], nt,
                                  preferred_element_type=F32)
            s_d = jnp.where(causal, s_d, -jnp.inf)
            mc = _lane_groups(s_d, jnp.maximum)
            if t:
                s_f = lax.dot_general(qr, kr_ref[0:r0, mp * A_QK:(mp + 1) * A_QK], nt,
                                      preferred_element_type=F32)
                mc = jnp.maximum(mc, _lane_groups(s_f, jnp.maximum))
            m = jnp.broadcast_to(jnp.max(mc, axis=1, keepdims=True), (tq, LANES))
            p_d = jnp.exp2(s_d - _lane_tile(m, tq))
            ps = _lane_groups(p_d, jnp.add)
            pd.append(p_d)
            if t:
                p_f = jnp.exp2(s_f - _lane_tile(m, r0))
                ps = ps + _lane_groups(p_f, jnp.add)
                pf.append(p_f)
            l = jnp.sum(ps, axis=1, keepdims=True)
            inv.append(jnp.broadcast_to((1.0 if mp == 0 else lam) / l, (tq, LANES)))
        a_d = pd[0] * _lane_tile(inv[0], tq) - pd[1] * _lane_tile(inv[1], tq)
        o = jnp.dot(a_d.astype(BF16), v_ref[r0:r0 + tq, :], preferred_element_type=F32)
        if t:
            a_f = pf[0] * _lane_tile(inv[0], r0) - pf[1] * _lane_tile(inv[1], r0)
            o = o + jnp.dot(a_f.astype(BF16), v_ref[0:r0, :], preferred_element_type=F32)
        ms = jnp.mean(o * o, axis=-1, keepdims=True)
        on = ((o * lax.rsqrt(ms + EPS)) * an_ref[...]) * (1.0 - lam_init)
        out_ref[r0:r0 + tq, :] = on.astype(out_ref.dtype)


def _attn(z, cos_t, sin_t, lam_qk, lam_init_arr, a_norm_all, layer, tq=A_BLOCK):
    return pl.pallas_call(
        functools.partial(_attn_kernel, tq=tq),
        out_shape=jax.ShapeDtypeStruct((TOKENS, A_HEADS * A_V), BF16),
        grid=(BATCH, A_HEADS),
        in_specs=[
            pl.BlockSpec((SEQ, 2 * A_QK), lambda b, h: (b, Z_AQ // 256 + h)),
            pl.BlockSpec((SEQ, 2 * A_QK), lambda b, h: (b, Z_AK // 256 + h)),
            pl.BlockSpec((SEQ, A_V), lambda b, h: (b, Z_AV // 256 + h)),
            pl.BlockSpec((SEQ, LANES), lambda b, h: (0, 0)),
            pl.BlockSpec((SEQ, LANES), lambda b, h: (0, 0)),
            pl.BlockSpec((None, 4, A_QK), lambda b, h: (layer, 0, 0)),
            pl.BlockSpec((None, 1, LANES), lambda b, h: (layer, 0, 0)),
            pl.BlockSpec((None, 1, A_V), lambda b, h: (layer, 0, h)),
        ],
        out_specs=pl.BlockSpec((SEQ, A_V), lambda b, h: (b, h)),
        scratch_shapes=[pltpu.VMEM((SEQ, 2 * A_QK), BF16)],
        compiler_params=_cparams(("parallel", "parallel")),
        name="diff_attn",
    )(z, z, z, cos_t, sin_t, lam_qk, lam_init_arr, a_norm_all)


def _mixout_kernel(hm_ref, ha_ref, gm_ref, gd_ref, wm_ref, wd_ref, wo_ref, h_ref, out_ref, y_ref, *, nsplit):
    cw = D_MODEL // nsplit
    hm = hm_ref[...]
    ha = ha_ref[...]
    for n in range(nsplit):
        c = slice(n * cw, (n + 1) * cw)
        ym = jnp.dot(hm, wm_ref[:, c], preferred_element_type=F32)
        yd = jnp.dot(ha, wd_ref[:, c], preferred_element_type=F32)
        y = (jax.nn.sigmoid(gm_ref[:, c].astype(F32)) * ym
             + jax.nn.sigmoid(gd_ref[:, c].astype(F32)) * yd)
        y_ref[:, c] = y.astype(y_ref.dtype)
    y = y_ref[...]
    for n in range(nsplit):
        c = slice(n * cw, (n + 1) * cw)
        out_ref[:, c] = h_ref[:, c] + jnp.dot(y, wo_ref[:, c], preferred_element_type=F32)


def _mixout(h, hm, ha, z, wbm_all, wbd_all, wo_all, layer, tm=256, nsplit=4):
    wspec = pl.BlockSpec((None, D_MODEL, D_MODEL), lambda i: (layer, 0, 0),
                         pipeline_mode=pl.Buffered(1))
    return pl.pallas_call(
        functools.partial(_mixout_kernel, nsplit=nsplit),
        out_shape=jax.ShapeDtypeStruct((TOKENS, D_MODEL), F32),
        grid=(TOKENS // tm,),
        in_specs=[
            pl.BlockSpec((tm, D_MODEL), lambda i: (i, 0)),
            pl.BlockSpec((tm, D_MODEL), lambda i: (i, 0)),
            pl.BlockSpec((tm, D_MODEL), lambda i: (i, Z_GT // D_MODEL)),
            pl.BlockSpec((tm, D_MODEL), lambda i: (i, Z_GT // D_MODEL + 1)),
            wspec, wspec, wspec,
            pl.BlockSpec((tm, D_MODEL), lambda i: (i, 0)),
        ],
        out_specs=pl.BlockSpec((tm, D_MODEL), lambda i: (i, 0)),
        scratch_shapes=[pltpu.VMEM((tm, D_MODEL), BF16)],
        compiler_params=_cparams(("parallel",)),
        name="mix_out",
    )(hm, ha, z, z, wbm_all, wbd_all, wo_all, h)


def _resproj_kernel(h_ref, y_ref, w_ref, out_ref):
    out_ref[...] = h_ref[...] + jnp.dot(y_ref[...], w_ref[...], preferred_element_type=F32)


def _resproj(h, y, w_all, layer, tm, tn, name):
    kdim = y.shape[1]
    return pl.pallas_call(
        _resproj_kernel,
        out_shape=jax.ShapeDtypeStruct((TOKENS, D_MODEL), F32),
        grid=(TOKENS // tm, D_MODEL // tn),
        in_specs=[
            pl.BlockSpec((tm, tn), lambda i, j: (i, j)),
            pl.BlockSpec((tm, kdim), lambda i, j: (i, 0)),
            pl.BlockSpec((None, kdim, tn), lambda i, j: (layer, 0, j)),
        ],
        out_specs=pl.BlockSpec((tm, tn), lambda i, j: (i, j)),
        compiler_params=_cparams(("parallel", "arbitrary")),
        name=name,
    )(h, y, w_all)


def _ffn_up_kernel(x_ref, g_ref, wg_ref, wv_ref, cw_ref, cb_ref, u_ref, xn_ref, carry_ref, *, tm, tn, sub):
    i = pl.program_id(0)
    j = pl.program_id(1)

    @pl.when(j == 0)
    def _():
        _rmsnorm_rows(x_ref, g_ref, xn_ref, tm)

    @pl.when((i % (SEQ // tm)) == 0)
    def _():
        carry_ref[j] = jnp.zeros((8, tn), F32)

    prev = carry_ref[j]
    cw = cw_ref[...]
    cb = cb_ref[...]
    rowi = lax.broadcasted_iota(jnp.int32, (sub, tn), 0)
    is0 = rowi == 0
    is1 = rowi == 1
    for r in range(tm // sub):
        xs = xn_ref[r * sub:(r + 1) * sub, :]
        gate = jnp.dot(xs, wg_ref[...], preferred_element_type=F32)
        val = jnp.dot(xs, wv_ref[...], preferred_element_type=F32)
        g1 = jnp.where(is0, prev[7:8, :], pltpu.roll(gate, 1, 0))
        g2 = jnp.where(is0, prev[6:7, :], jnp.where(is1, prev[7:8, :], pltpu.roll(gate, 2, 0)))
        conv = cw[0:1, :] * g2 + cw[1:2, :] * g1 + cw[2:3, :] * gate + cb
        u_ref[r * sub:(r + 1) * sub, :] = ((conv * jax.nn.sigmoid(conv)) * val).astype(u_ref.dtype)
        prev = gate[sub - 8:sub, :]
    carry_ref[j] = prev


def _ffn_up(h, g_all, wup_all, cw_all, cb_all, layer, tm=1024, tn=512, sub=256):
    nf = D_FF // tn
    return pl.pallas_call(
        functools.partial(_ffn_up_kernel, tm=tm, tn=tn, sub=sub),
        out_shape=jax.ShapeDtypeStruct((TOKENS, D_FF), BF16),
        grid=(TOKENS // tm, nf),
        in_specs=[
            pl.BlockSpec((tm, D_MODEL), lambda i, j: (i, 0)),
            pl.BlockSpec((None, 1, D_MODEL), lambda i, j: (layer, 0, 0)),
            pl.BlockSpec((None, D_MODEL, tn), lambda i, j: (layer, 0, j)),
            pl.BlockSpec((None, D_MODEL, tn), lambda i, j: (layer, 0, nf + j)),
            pl.BlockSpec((None, CONV_W, tn), lambda i, j: (layer, 0, j)),
            pl.BlockSpec((None, 1, tn), lambda i, j: (layer, 0, j)),
        ],
        out_specs=pl.BlockSpec((tm, tn), lambda i, j: (i, j)),
        scratch_shapes=[pltpu.VMEM((tm, D_MODEL), BF16),
                        pltpu.VMEM((nf, 8, tn), F32)],
        compiler_params=_cparams(("arbitrary", "arbitrary")),
        name="ffn_up",
    )(h, g_all, wup_all, wup_all, cw_all, cb_all)


def _final_norm_kernel(x_ref, g_ref, out_ref, *, tm):
    _rmsnorm_rows(x_ref, g_ref, out_ref, tm)


def _final_norm(h, g, tm=512):
    return pl.pallas_call(
        functools.partial(_final_norm_kernel, tm=tm),
        out_shape=jax.ShapeDtypeStruct((TOKENS, D_MODEL), F32),
        grid=(TOKENS // tm,),
        in_specs=[pl.BlockSpec((tm, D_MODEL), lambda i: (i, 0)),
                  pl.BlockSpec((1, D_MODEL), lambda i: (0, 0))],
        out_specs=pl.BlockSpec((tm, D_MODEL), lambda i: (i, 0)),
        compiler_params=_cparams(("parallel",)),
        name="final_norm",
    )(h, g)


def _rope_tables():
    half = ROPE_DIM // 2
    inv = ROPE_THETA ** (-jnp.arange(half, dtype=F32) / half)
    ang = jnp.arange(SEQ).astype(F32)[:, None] * inv[None, :]
    cos, sin = jnp.cos(ang), jnp.sin(ang)
    ones = jnp.ones((SEQ, LANES - ROPE_DIM), F32)
    zeros = jnp.zeros((SEQ, LANES - ROPE_DIM), F32)
    return (jnp.concatenate([cos, cos, ones], axis=1),
            jnp.concatenate([-sin, sin, zeros], axis=1))


def kernel(x, norm_mix, w_in, b_in, m_norm, a_norm, lam_qk, w_bm, w_bd, w_o, norm_ffn, w_up, conv_w, conv_b, w_down, norm_final):
    w_main = jnp.concatenate([w_in[:, :, :GATE_LO], w_in[:, :, GATE_HI:]], axis=2).astype(BF16)
    b_main = jnp.concatenate([b_in[:, :GATE_LO], b_in[:, GATE_HI:]], axis=1)[:, None, :]
    pad = LANES - (GATE_HI - GATE_LO)
    w_gate = jnp.pad(w_in[:, :, GATE_LO:GATE_HI], ((0, 0), (0, 0), (0, pad))).astype(BF16)
    b_gate = jnp.pad(b_in[:, GATE_LO:GATE_HI], ((0, 0), (0, pad)))[:, None, :]
    w_bm_b, w_bd_b, w_o_b = w_bm.astype(BF16), w_bd.astype(BF16), w_o.astype(BF16)
    w_up_b, w_down_b = w_up.astype(BF16), w_down.astype(BF16)
    norm_mix3, norm_ffn3 = norm_mix[:, None, :], norm_ffn[:, None, :]
    m_norm3, a_norm3 = m_norm[:, None, :], a_norm[:, None, :]
    conv_b3 = conv_b[:, None, :]
    lam_init = jnp.asarray([0.8 - 0.6 * math.exp(-0.3 * l) for l in range(DEPTH)], F32)
    lam_init3 = jnp.broadcast_to(lam_init[:, None, None], (DEPTH, 1, LANES))
    cos_t, sin_t = _rope_tables()

    h = x.reshape(TOKENS, D_MODEL)
    for l in range(DEPTH):
        z, zg = _inproj(h, norm_mix3, w_main, b_main, w_gate, b_gate, l)
        hm = _mlstm(z, zg, m_norm3, l)
        ha = _attn(z, cos_t, sin_t, lam_qk, lam_init3, a_norm3, l)
        h = _mixout(h, hm, ha, z, w_bm_b, w_bd_b, w_o_b, l)
        u = _ffn_up(h, norm_ffn3, w_up_b, conv_w, conv_b3, l)
        h = _resproj(h, u, w_down_b, l, tm=1024, tn=512, name="ffn_down")
    out = _final_norm(h, norm_final[None, :])
    return out.reshape(BATCH, SEQ, D_MODEL)
```

```python
import functools
import math

import jax
import jax.numpy as jnp
import numpy as np
from jax import lax
from jax.experimental import pallas as pl
from jax.experimental.pallas import tpu as pltpu

F32 = jnp.float32
BF16 = jnp.bfloat16

D_MODEL = 2048
BATCH = 4
SEQ = 2048
DEPTH = 4
TOKENS = BATCH * SEQ

M_HEADS = 8
M_QK = 128
M_V = 256
A_HEADS = 8
A_QK = 128
A_V = 256
ROPE_DIM = A_QK // 4
ROPE_THETA = 500000.0
D_FF = 5632
CONV_W = 3
EPS = 1e-6

_SPLITS = [M_HEADS * M_QK, M_HEADS * M_QK, M_HEADS * M_V, M_HEADS * M_V, M_HEADS, M_HEADS,
           2 * A_HEADS * A_QK, 2 * A_HEADS * A_QK, A_HEADS * A_V, 2 * D_MODEL]
_OFFS = [0] + [int(o) for o in np.cumsum(_SPLITS)]
GATE_LO, GATE_HI = _OFFS[4], _OFFS[6]
Z_Q, Z_K, Z_V, Z_O = 0, 1024, 2048, 4096
Z_AQ, Z_AK, Z_AV, Z_GT = 6144, 8192, 10240, 12288
N_Z = 16384
LANES = 128

VMEM_LIMIT = 56 * 1024 * 1024

M_CHUNK = 256
A_BLOCK = 512


def _cparams(sem):
    return pltpu.CompilerParams(dimension_semantics=sem, vmem_limit_bytes=VMEM_LIMIT)


def _rmsnorm_rows(x_ref, g_ref, out_ref, rows, chunk=256):
    def body(c, carry):
        r = pl.multiple_of(c * chunk, chunk)
        x = x_ref[pl.ds(r, chunk), :]
        ms = jnp.mean(x * x, axis=-1, keepdims=True)
        out_ref[pl.ds(r, chunk), :] = ((x * lax.rsqrt(ms + EPS)) * g_ref[...]).astype(out_ref.dtype)
        return carry
    lax.fori_loop(0, rows // chunk, body, 0)


def _stage_win_kernel(a_ref, b_ref, wm_ref, wg_ref, *, tn):
    j = pl.program_id(1)
    n_lo = GATE_LO // tn
    shift = GATE_HI - GATE_LO
    ngrp = tn // LANES
    rchunk = 256

    @pl.when(j < n_lo)
    def _():
        wm_ref[...] = a_ref[...].astype(BF16)

    @pl.when(j >= n_lo)
    def _():
        keep = lax.broadcasted_iota(jnp.int32, (rchunk, LANES), 1) < LANES - shift
        for r in range(D_MODEL // rchunk):
            rows = slice(r * rchunk, (r + 1) * rchunk)
            src = [a_ref[rows, g * LANES:(g + 1) * LANES] for g in range(ngrp)] + [b_ref[rows, :]]
            rolled = [pltpu.roll(x, LANES - shift, 1) for x in src]
            for g in range(ngrp):
                wm_ref[rows, g * LANES:(g + 1) * LANES] = jnp.where(
                    keep, rolled[g], rolled[g + 1]).astype(BF16)

    @pl.when(j == n_lo)
    def _():
        head = lax.broadcasted_iota(jnp.int32, (D_MODEL, LANES), 1) < M_HEADS
        gcols = a_ref[:, 0:LANES]
        wg_ref[:, 0:LANES] = jnp.where(head, gcols, 0.0).astype(BF16)
        wg_ref[:, LANES:2 * LANES] = jnp.where(
            head, pltpu.roll(gcols, LANES - M_HEADS, 1), 0.0).astype(BF16)


def _stage_win(w_in, tn=512):
    return pl.pallas_call(
        functools.partial(_stage_win_kernel, tn=tn),
        out_shape=(jax.ShapeDtypeStruct((DEPTH, D_MODEL, N_Z), BF16),
                   jax.ShapeDtypeStruct((DEPTH, D_MODEL, 2 * LANES), BF16)),
        grid=(DEPTH, N_Z // tn),
        in_specs=[
            pl.BlockSpec((None, D_MODEL, tn), lambda l, j: (l, 0, j)),
            pl.BlockSpec((None, D_MODEL, LANES), lambda l, j: (l, 0, (j + 1) * (tn // LANES))),
        ],
        out_specs=(pl.BlockSpec((None, D_MODEL, tn), lambda l, j: (l, 0, j)),
                   pl.BlockSpec((None, D_MODEL, 2 * LANES), lambda l, j: (l, 0, 0))),
        compiler_params=_cparams(("arbitrary", "arbitrary")),
        name="stage_win",
    )(w_in, w_in)


def _norm_kernel(x_ref, g_ref, out_ref, *, tm):
    _rmsnorm_rows(x_ref, g_ref, out_ref, tm)


def _norm(x, g_all, layer, tm=512):
    return pl.pallas_call(
        functools.partial(_norm_kernel, tm=tm),
        out_shape=jax.ShapeDtypeStruct((TOKENS, D_MODEL), BF16),
        grid=(TOKENS // tm,),
        in_specs=[pl.BlockSpec((tm, D_MODEL), lambda i: (i, 0)),
                  pl.BlockSpec((None, 1, D_MODEL), lambda i: (layer, 0, 0))],
        out_specs=pl.BlockSpec((tm, D_MODEL), lambda i: (i, 0)),
        compiler_params=_cparams(("parallel",)),
        name="norm_in",
    )(x, g_all)


def _inproj_kernel(xn_ref, w_ref, b_ref, wg_ref, bg_ref, z_ref, zg_ref, *, tn, nsplit):
    @pl.when(pl.program_id(1) == 0)
    def _():
        zg_ref[...] = jnp.dot(xn_ref[...], wg_ref[...], preferred_element_type=F32) + bg_ref[...]

    xn = xn_ref[...]
    cw = tn // nsplit
    for n in range(nsplit):
        c = slice(n * cw, (n + 1) * cw)
        z_ref[:, c] = (jnp.dot(xn, w_ref[:, c], preferred_element_type=F32)
                       + b_ref[:, c]).astype(z_ref.dtype)


def _inproj(xn, w_all, b_all, wg_all, bg_all, layer, tm=1024, tn=2048, nsplit=2):
    return pl.pallas_call(
        functools.partial(_inproj_kernel, tn=tn, nsplit=nsplit),
        out_shape=(jax.ShapeDtypeStruct((TOKENS, N_Z), BF16),
                   jax.ShapeDtypeStruct((TOKENS, 2 * LANES), F32)),
        grid=(TOKENS // tm, N_Z // tn),
        in_specs=[
            pl.BlockSpec((tm, D_MODEL), lambda i, j: (i, 0)),
            pl.BlockSpec((None, D_MODEL, tn), lambda i, j: (layer, 0, j)),
            pl.BlockSpec((None, 1, tn), lambda i, j: (layer, 0, j)),
            pl.BlockSpec((None, D_MODEL, 2 * LANES), lambda i, j: (layer, 0, 0)),
            pl.BlockSpec((None, 1, 2 * LANES), lambda i, j: (layer, 0, 0)),
        ],
        out_specs=(pl.BlockSpec((tm, tn), lambda i, j: (i, j)),
                   pl.BlockSpec((tm, 2 * LANES), lambda i, j: (i, 0))),
        compiler_params=_cparams(("parallel", "arbitrary")),
        name="inproj",
    )(xn, w_all, b_all, wg_all, bg_all)


def _log_sigmoid(x):
    return -(jnp.maximum(-x, 0.0) + jnp.log1p(jnp.exp(-jnp.abs(x))))


def _cummax_rows(x, rows):
    ridx = lax.broadcasted_iota(jnp.int32, x.shape, 0)
    k = 1
    while k < 8:
        x = jnp.maximum(x, jnp.where(ridx < k, -jnp.inf, pltpu.roll(x, k, 0)))
        k *= 2
    while k < rows:
        pad = jnp.full((k, x.shape[1]), -jnp.inf, x.dtype)
        x = jnp.maximum(x, jnp.concatenate([pad, x[:rows - k]], axis=0))
        k *= 2
    return x


def _mlstm_kernel(q_ref, k_ref, v_ref, o_ref, zg_ref, mn_ref, out_ref, ct_ref, m_ref, *, L):
    @pl.when(pl.program_id(1) == 0)
    def _():
        ct_ref[...] = jnp.zeros_like(ct_ref)
        m_ref[...] = jnp.zeros_like(m_ref)

    scale = M_QK ** -0.5
    log2e = math.log2(math.e)
    row = lax.broadcasted_iota(jnp.int32, (L, L), 0)
    col = lax.broadcasted_iota(jnp.int32, (L, L), 1)
    causal = row >= col
    lf = _log_sigmoid(zg_ref[:, LANES:2 * LANES])
    b = jnp.dot(causal.astype(F32), lf, precision=lax.Precision.HIGHEST,
                preferred_element_type=F32)
    a = zg_ref[:, 0:LANES] - b
    m_prev = m_ref[0:1, :]
    big_m = jnp.maximum(m_prev, _cummax_rows(a, L))
    a2_t = (a * log2e).T
    m2 = big_m * log2e - math.log2(scale)
    e_inter = jnp.exp(m_prev - big_m) * scale
    e_negm = jnp.exp(-(b + big_m))
    m_last = big_m[L - 1:L, :]
    w_all = jnp.exp(a - m_last)
    decay = jnp.exp(m_prev - m_last)
    m_ref[0:1, :] = b[L - 1:L, :] + m_last
    ones_blk = jnp.ones((L, LANES), BF16)
    tn_dims = (((0,), (0,)), ((), ()))

    for h in range(M_HEADS):
        q = q_ref[:, h * M_QK:(h + 1) * M_QK]
        k = k_ref[:, h * M_QK:(h + 1) * M_QK]
        v = v_ref[:, h * M_V:(h + 1) * M_V]
        m2_b = jnp.broadcast_to(m2[:, h:h + 1], (L, LANES))
        arg = jnp.where(causal, a2_t[h:h + 1, :] - _lane_tile(m2_b, L), -jnp.inf)
        qk = lax.dot_general(q, k, (((1,), (1,)), ((), ())), preferred_element_type=F32)
        s_mat = qk * jnp.exp2(arg)
        ct = ct_ref[h]
        qc = jnp.dot(q, ct.astype(BF16), preferred_element_type=F32)
        inter = e_inter[:, h:h + 1]
        num = (jnp.dot(s_mat.astype(BF16), v, preferred_element_type=F32)
               + jnp.broadcast_to(inter, (L, M_V)) * qc[:, :M_V])
        den = jnp.sum(s_mat, axis=1, keepdims=True) + inter * qc[:, M_V:M_V + 1]
        rden = 1.0 / jnp.maximum(jnp.abs(den), e_negm[:, h:h + 1])
        hh = num * rden
        ms = jnp.mean(hh * hh, axis=-1, keepdims=True)
        hn = (hh * lax.rsqrt(ms + EPS)) * mn_ref[:, h * M_V:(h + 1) * M_V]
        og = o_ref[:, h * M_V:(h + 1) * M_V].astype(F32)
        out_ref[:, h * M_V:(h + 1) * M_V] = (hn * jax.nn.sigmoid(og)).astype(out_ref.dtype)

        kw = (jnp.broadcast_to(w_all[:, h:h + 1], (L, M_QK)) * k.astype(F32)).astype(BF16)
        upd = jnp.concatenate(
            [lax.dot_general(kw, v, tn_dims, preferred_element_type=F32),
             lax.dot_general(kw, ones_blk, tn_dims, preferred_element_type=F32)], axis=1)
        ct_ref[h] = decay[:, h:h + 1] * ct + upd


def _mlstm(z, zg, m_norm_all, layer, L=M_CHUNK):
    nc = SEQ // L
    row = lambda b, c: b * nc + c
    return pl.pallas_call(
        functools.partial(_mlstm_kernel, L=L),
        out_shape=jax.ShapeDtypeStruct((TOKENS, M_HEADS * M_V), BF16),
        grid=(BATCH, nc),
        in_specs=[
            pl.BlockSpec((L, 1024), lambda b, c: (row(b, c), Z_Q // 1024)),
            pl.BlockSpec((L, 1024), lambda b, c: (row(b, c), Z_K // 1024)),
            pl.BlockSpec((L, 2048), lambda b, c: (row(b, c), Z_V // 2048)),
            pl.BlockSpec((L, 2048), lambda b, c: (row(b, c), Z_O // 2048)),
            pl.BlockSpec((L, 2 * LANES), lambda b, c: (row(b, c), 0)),
            pl.BlockSpec((None, 1, M_HEADS * M_V), lambda b, c: (layer, 0, 0)),
        ],
        out_specs=pl.BlockSpec((L, M_HEADS * M_V), lambda b, c: (row(b, c), 0)),
        scratch_shapes=[pltpu.VMEM((M_HEADS, M_QK, M_V + LANES), F32),
                        pltpu.VMEM((M_HEADS, LANES), F32)],
        compiler_params=_cparams(("parallel", "arbitrary")),
        name="mlstm",
    )(z, z, z, z, zg, m_norm_all)


def _rope(x, cos, sin):
    half = ROPE_DIM // 2
    lane = lax.broadcasted_iota(jnp.int32, x.shape, 1)
    partner = jnp.where(lane < half, pltpu.roll(x, LANES - half, 1), pltpu.roll(x, half, 1))
    return x * cos + partner * sin


def _lane_groups(x, op):
    acc = x[:, 0:LANES]
    for g in range(1, x.shape[1] // LANES):
        acc = op(acc, x[:, g * LANES:(g + 1) * LANES])
    return acc


def _lane_tile(a, width):
    return jnp.concatenate([a] * (width // LANES), axis=1) if width > LANES else a


def _attn_kernel(q_ref, k_ref, v_ref, cos_ref, sin_ref, lq_ref, li_ref, an_ref, out_ref, kr_ref, *, tq):
    qscale = (A_QK ** -0.5) * math.log2(math.e)
    nt = (((1,), (1,)), ((), ()))
    rchunk = 256
    for c in range(SEQ // rchunk):
        r = c * rchunk
        kk = k_ref[r:r + rchunk, :].astype(F32)
        cos = cos_ref[r:r + rchunk, :]
        sin = sin_ref[r:r + rchunk, :]
        for mp in range(2):
            kr_ref[r:r + rchunk, mp * A_QK:(mp + 1) * A_QK] = _rope(
                kk[:, mp * A_QK:(mp + 1) * A_QK], cos, sin).astype(BF16)

    lq = lq_ref[...]
    lam_init = li_ref[0:1, 0:1]
    lam = (jnp.exp(jnp.sum(lq[0:1] * lq[1:2], axis=1, keepdims=True))
           - jnp.exp(jnp.sum(lq[2:3] * lq[3:4], axis=1, keepdims=True)) + lam_init)
    causal = (lax.broadcasted_iota(jnp.int32, (tq, tq), 0)
              >= lax.broadcasted_iota(jnp.int32, (tq, tq), 1))

    for t in range(SEQ // tq):
        r0 = t * tq
        qf = q_ref[r0:r0 + tq, :].astype(F32)
        cos_q = cos_ref[r0:r0 + tq, :]
        sin_q = sin_ref[r0:r0 + tq, :]
        pd, pf, lsum = [], [], []
        for mp in range(2):
            qr = (_rope(qf[:, mp * A_QK:(mp + 1) * A_QK], cos_q, sin_q) * qscale).astype(BF16)
            s_d = lax.dot_general(qr, kr_ref[r0:r0 + tq, mp * A_QK:(mp + 1) * A_QK], nt,
                                  preferred_element_type=F32)
            s_d = jnp.where(causal, s_d, -jnp.inf)
            mc = _lane_groups(s_d, jnp.maximum)
            if t:
                s_f = lax.dot_general(qr, kr_ref[0:r0, mp * A_QK:(mp + 1) * A_QK], nt,
                                      preferred_element_type=F32)
                mc = jnp.maximum(mc, _lane_groups(s_f, jnp.maximum))
            m = jnp.broadcast_to(jnp.max(mc, axis=1, keepdims=True), (tq, LANES))
            p_d = jnp.exp2(s_d - _lane_tile(m, tq))
            ps = _lane_groups(p_d, jnp.add)
            pd.append(p_d)
            if t:
                p_f = jnp.exp2(s_f - _lane_tile(m, r0))
                ps = ps + _lane_groups(p_f, jnp.add)
                pf.append(p_f)
            lsum.append(jnp.sum(ps, axis=1, keepdims=True))
        ratio = jnp.broadcast_to(lam * lsum[0] / lsum[1], (tq, LANES))
        a_d = pd[0] - pd[1] * _lane_tile(ratio, tq)
        o = jnp.dot(a_d.astype(BF16), v_ref[r0:r0 + tq, :], preferred_element_type=F32)
        if t:
            a_f = pf[0] - pf[1] * _lane_tile(ratio, r0)
            o = o + jnp.dot(a_f.astype(BF16), v_ref[0:r0, :], preferred_element_type=F32)
        o = o / lsum[0]
        ms = jnp.mean(o * o, axis=-1, keepdims=True)
        on = ((o * lax.rsqrt(ms + EPS)) * an_ref[...]) * (1.0 - lam_init)
        out_ref[r0:r0 + tq, :] = on.astype(out_ref.dtype)


def _attn(z, cos_t, sin_t, lam_qk, lam_init_arr, a_norm_all, layer, tq=A_BLOCK):
    return pl.pallas_call(
        functools.partial(_attn_kernel, tq=tq),
        out_shape=jax.ShapeDtypeStruct((TOKENS, A_HEADS * A_V), BF16),
        grid=(BATCH, A_HEADS),
        in_specs=[
            pl.BlockSpec((SEQ, 2 * A_QK), lambda b, h: (b, Z_AQ // 256 + h)),
            pl.BlockSpec((SEQ, 2 * A_QK), lambda b, h: (b, Z_AK // 256 + h)),
            pl.BlockSpec((SEQ, A_V), lambda b, h: (b, Z_AV // 256 + h)),
            pl.BlockSpec((SEQ, LANES), lambda b, h: (0, 0)),
            pl.BlockSpec((SEQ, LANES), lambda b, h: (0, 0)),
            pl.BlockSpec((None, 4, A_QK), lambda b, h: (layer, 0, 0)),
            pl.BlockSpec((None, 1, LANES), lambda b, h: (layer, 0, 0)),
            pl.BlockSpec((None, 1, A_V), lambda b, h: (layer, 0, h)),
        ],
        out_specs=pl.BlockSpec((SEQ, A_V), lambda b, h: (b, h)),
        scratch_shapes=[pltpu.VMEM((SEQ, 2 * A_QK), BF16)],
        compiler_params=_cparams(("parallel", "parallel")),
        name="diff_attn",
    )(z, z, z, cos_t, sin_t, lam_qk, lam_init_arr, a_norm_all)


def _mixout_kernel(hm_ref, ha_ref, gm_ref, gd_ref, wm_ref, wd_ref, wo_ref, h_ref, g_ref,
                   out_ref, xn_ref, y_ref, *, tm, nsplit):
    cw = D_MODEL // nsplit
    hm = hm_ref[...]
    ha = ha_ref[...]
    for n in range(nsplit):
        c = slice(n * cw, (n + 1) * cw)
        ym = jnp.dot(hm, wm_ref[:, c], preferred_element_type=F32)
        yd = jnp.dot(ha, wd_ref[:, c], preferred_element_type=F32)
        y = (jax.nn.sigmoid(gm_ref[:, c].astype(F32)) * ym
             + jax.nn.sigmoid(gd_ref[:, c].astype(F32)) * yd)
        y_ref[:, c] = y.astype(y_ref.dtype)
    y = y_ref[...]
    for n in range(nsplit):
        c = slice(n * cw, (n + 1) * cw)
        out_ref[:, c] = h_ref[:, c] + jnp.dot(y, wo_ref[:, c], preferred_element_type=F32)
    _rmsnorm_rows(out_ref, g_ref, xn_ref, tm, chunk=tm)


def _mixout(h, hm, ha, z, wbm_all, wbd_all, wo_all, g_all, layer, tm=256, nsplit=4):
    wspec = pl.BlockSpec((None, D_MODEL, D_MODEL), lambda i: (layer, 0, 0),
                         pipeline_mode=pl.Buffered(1))
    row = pl.BlockSpec((tm, D_MODEL), lambda i: (i, 0))
    return pl.pallas_call(
        functools.partial(_mixout_kernel, tm=tm, nsplit=nsplit),
        out_shape=(jax.ShapeDtypeStruct((TOKENS, D_MODEL), F32),
                   jax.ShapeDtypeStruct((TOKENS, D_MODEL), BF16)),
        grid=(TOKENS // tm,),
        in_specs=[
            row, row,
            pl.BlockSpec((tm, D_MODEL), lambda i: (i, Z_GT // D_MODEL)),
            pl.BlockSpec((tm, D_MODEL), lambda i: (i, Z_GT // D_MODEL + 1)),
            wspec, wspec, wspec,
            row,
            pl.BlockSpec((None, 1, D_MODEL), lambda i: (layer, 0, 0)),
        ],
        out_specs=(row, row),
        scratch_shapes=[pltpu.VMEM((tm, D_MODEL), BF16)],
        compiler_params=_cparams(("parallel",)),
        name="mix_out",
    )(hm, ha, z, z, wbm_all, wbd_all, wo_all, h, g_all)


def _ffn_up_kernel(xn_ref, wg_ref, wv_ref, cw_ref, cb_ref, u_ref, wgb_ref, wvb_ref, carry_ref,
                   *, tm, tn, sub):
    i = pl.program_id(1)

    @pl.when(i == 0)
    def _():
        wgb_ref[...] = wg_ref[...].astype(BF16)
        wvb_ref[...] = wv_ref[...].astype(BF16)

    @pl.when((i % (SEQ // tm)) == 0)
    def _():
        carry_ref[...] = jnp.zeros_like(carry_ref)

    prev = carry_ref[...]
    cw = cw_ref[...]
    cb = cb_ref[...]
    rowi = lax.broadcasted_iota(jnp.int32, (sub, tn), 0)
    is0 = rowi == 0
    is1 = rowi == 1
    for r in range(tm // sub):
        xs = xn_ref[r * sub:(r + 1) * sub, :]
        gate = jnp.dot(xs, wgb_ref[...], preferred_element_type=F32)
        val = jnp.dot(xs, wvb_ref[...], preferred_element_type=F32)
        g1 = jnp.where(is0, prev[7:8, :], pltpu.roll(gate, 1, 0))
        g2 = jnp.where(is0, prev[6:7, :], jnp.where(is1, prev[7:8, :], pltpu.roll(gate, 2, 0)))
        conv = cw[0:1, :] * g2 + cw[1:2, :] * g1 + cw[2:3, :] * gate + cb
        u_ref[r * sub:(r + 1) * sub, :] = ((conv * jax.nn.sigmoid(conv)) * val).astype(u_ref.dtype)
        prev = gate[sub - 8:sub, :]
    carry_ref[...] = prev


def _ffn_up(xn, wup_all, cw_all, cb_all, layer, tm=1024, tn=512, sub=256):
    nf = D_FF // tn
    return pl.pallas_call(
        functools.partial(_ffn_up_kernel, tm=tm, tn=tn, sub=sub),
        out_shape=jax.ShapeDtypeStruct((TOKENS, D_FF), BF16),
        grid=(nf, TOKENS // tm),
        in_specs=[
            pl.BlockSpec((tm, D_MODEL), lambda j, i: (i, 0)),
            pl.BlockSpec((None, D_MODEL, tn), lambda j, i: (layer, 0, j)),
            pl.BlockSpec((None, D_MODEL, tn), lambda j, i: (layer, 0, nf + j)),
            pl.BlockSpec((None, CONV_W, tn), lambda j, i: (layer, 0, j)),
            pl.BlockSpec((None, 1, tn), lambda j, i: (layer, 0, j)),
        ],
        out_specs=pl.BlockSpec((tm, tn), lambda j, i: (i, j)),
        scratch_shapes=[pltpu.VMEM((D_MODEL, tn), BF16),
                        pltpu.VMEM((D_MODEL, tn), BF16),
                        pltpu.VMEM((8, tn), F32)],
        compiler_params=_cparams(("arbitrary", "arbitrary")),
        name="ffn_up",
    )(xn, wup_all, wup_all, cw_all, cb_all)


def _ffn_down_kernel(u_ref, w_ref, h_ref, g_ref, *out_refs, tm, nsplit, last):
    cw = D_MODEL // nsplit
    u = u_ref[...]
    if last:
        (y_ref,) = out_refs
        hn_ref = y_ref
    else:
        hn_ref, y_ref = out_refs
    for n in range(nsplit):
        c = slice(n * cw, (n + 1) * cw)
        hn_ref[:, c] = h_ref[:, c] + jnp.dot(u, w_ref[:, c], preferred_element_type=F32)
    _rmsnorm_rows(hn_ref, g_ref, y_ref, tm, chunk=tm)


def _ffn_down(h, u, w_all, g, layer, last, tm=256, nsplit=4):
    row = pl.BlockSpec((tm, D_MODEL), lambda i: (i, 0))
    if last:
        out_shape = jax.ShapeDtypeStruct((TOKENS, D_MODEL), F32)
        out_specs = row
    else:
        out_shape = (jax.ShapeDtypeStruct((TOKENS, D_MODEL), F32),
                     jax.ShapeDtypeStruct((TOKENS, D_MODEL), BF16))
        out_specs = (row, row)
    return pl.pallas_call(
        functools.partial(_ffn_down_kernel, tm=tm, nsplit=nsplit, last=last),
        out_shape=out_shape,
        grid=(TOKENS // tm,),
        in_specs=[
            pl.BlockSpec((tm, D_FF), lambda i: (i, 0)),
            pl.BlockSpec((None, D_FF, D_MODEL), lambda i: (layer, 0, 0),
                         pipeline_mode=pl.Buffered(1)),
            row,
            pl.BlockSpec((1, D_MODEL), lambda i: (0, 0)),
        ],
        out_specs=out_specs,
        compiler_params=_cparams(("parallel",)),
        name="ffn_down",
    )(u, w_all, h, g)


def _rope_tables():
    half = ROPE_DIM // 2
    inv = ROPE_THETA ** (-jnp.arange(half, dtype=F32) / half)
    ang = jnp.arange(SEQ).astype(F32)[:, None] * inv[None, :]
    cos, sin = jnp.cos(ang), jnp.sin(ang)
    ones = jnp.ones((SEQ, LANES - ROPE_DIM), F32)
    zeros = jnp.zeros((SEQ, LANES - ROPE_DIM), F32)
    return (jnp.concatenate([cos, cos, ones], axis=1),
            jnp.concatenate([-sin, sin, zeros], axis=1))


def kernel(x, norm_mix, w_in, b_in, m_norm, a_norm, lam_qk, w_bm, w_bd, w_o, norm_ffn, w_up, conv_w, conv_b, w_down, norm_final):
    w_main, w_gate = _stage_win(w_in)
    b_main = jnp.concatenate([b_in[:, :GATE_LO], b_in[:, GATE_HI:]], axis=1)[:, None, :]
    pad = ((0, 0), (0, LANES - M_HEADS))
    b_gate = jnp.concatenate([jnp.pad(b_in[:, GATE_LO:GATE_LO + M_HEADS], pad),
                              jnp.pad(b_in[:, GATE_LO + M_HEADS:GATE_HI], pad)], axis=1)[:, None, :]
    w_bm_b, w_bd_b, w_o_b = w_bm.astype(BF16), w_bd.astype(BF16), w_o.astype(BF16)
    w_down_b = w_down.astype(BF16)
    norm_mix3, norm_ffn3 = norm_mix[:, None, :], norm_ffn[:, None, :]
    m_norm3, a_norm3 = m_norm[:, None, :], a_norm[:, None, :]
    conv_b3 = conv_b[:, None, :]
    lam_init = jnp.asarray([0.8 - 0.6 * math.exp(-0.3 * l) for l in range(DEPTH)], F32)
    lam_init3 = jnp.broadcast_to(lam_init[:, None, None], (DEPTH, 1, LANES))
    cos_t, sin_t = _rope_tables()

    h = x.reshape(TOKENS, D_MODEL)
    xn = _norm(h, norm_mix3, 0)
    for l in range(DEPTH):
        last = l == DEPTH - 1
        z, zg = _inproj(xn, w_main, b_main, w_gate, b_gate, l)
        hm = _mlstm(z, zg, m_norm3, l)
        ha = _attn(z, cos_t, sin_t, lam_qk, lam_init3, a_norm3, l)
        h, xn = _mixout(h, hm, ha, z, w_bm_b, w_bd_b, w_o_b, norm_ffn3, l)
        u = _ffn_up(xn, w_up, conv_w, conv_b3, l)
        g_next = norm_final[None, :] if last else norm_mix[l + 1][None, :]
        res = _ffn_down(h, u, w_down_b, g_next, l, last)
        if last:
            out = res
        else:
            h, xn = res
    return out.reshape(BATCH, SEQ, D_MODEL)
```

```python
import functools
import math

import jax
import jax.numpy as jnp
import numpy as np
from jax import lax
from jax.experimental import pallas as pl
from jax.experimental.pallas import tpu as pltpu

F32 = jnp.float32
BF16 = jnp.bfloat16

D_MODEL = 2048
BATCH = 4
SEQ = 2048
DEPTH = 4
TOKENS = BATCH * SEQ

M_HEADS = 8
M_QK = 128
M_V = 256
A_HEADS = 8
A_QK = 128
A_V = 256
ROPE_DIM = A_QK // 4
ROPE_THETA = 500000.0
D_FF = 5632
CONV_W = 3
EPS = 1e-6

_SPLITS = [M_HEADS * M_QK, M_HEADS * M_QK, M_HEADS * M_V, M_HEADS * M_V, M_HEADS, M_HEADS,
           2 * A_HEADS * A_QK, 2 * A_HEADS * A_QK, A_HEADS * A_V, 2 * D_MODEL]
_OFFS = [0] + [int(o) for o in np.cumsum(_SPLITS)]
GATE_LO, GATE_HI = _OFFS[4], _OFFS[6]
Z_Q, Z_K, Z_V, Z_O = 0, 1024, 2048, 4096
Z_AQ, Z_AK, Z_AV, Z_GT = 6144, 8192, 10240, 12288
N_Z = 16384
LANES = 128

VMEM_LIMIT = 56 * 1024 * 1024

M_CHUNK = 256
A_BLOCK = 512


def _cparams(sem):
    return pltpu.CompilerParams(dimension_semantics=sem, vmem_limit_bytes=VMEM_LIMIT)


def _rmsnorm_rows(x_ref, g_ref, out_ref, rows, chunk=256):
    def body(c, carry):
        r = pl.multiple_of(c * chunk, chunk)
        x = x_ref[pl.ds(r, chunk), :]
        ms = jnp.mean(x * x, axis=-1, keepdims=True)
        out_ref[pl.ds(r, chunk), :] = ((x * lax.rsqrt(ms + EPS)) * g_ref[...]).astype(out_ref.dtype)
        return carry
    lax.fori_loop(0, rows // chunk, body, 0)


_NT_DIMS = (((1,), (1,)), ((), ()))


def _norm_kernel(x_ref, g_ref, out_ref, *, tm):
    _rmsnorm_rows(x_ref, g_ref, out_ref, tm)


def _norm(x, g_all, layer, tm=512):
    row = pl.BlockSpec((tm, D_MODEL), lambda i: (i, 0))
    return pl.pallas_call(
        functools.partial(_norm_kernel, tm=tm),
        out_shape=jax.ShapeDtypeStruct((TOKENS, D_MODEL), BF16),
        grid=(TOKENS // tm,),
        in_specs=[row, pl.BlockSpec((None, 1, D_MODEL), lambda i: (layer, 0, 0))],
        out_specs=row,
        compiler_params=_cparams(("parallel",)),
        name="norm_in",
    )(x, g_all)


def _inproj_kernel(xn_ref, wa_ref, wb_ref, b_ref, z_ref, wbf_ref, *, tn):
    j = pl.program_id(0)
    n_lo = GATE_LO // tn
    shift = GATE_HI - GATE_LO

    @pl.when((pl.program_id(1) == 0) & (j < n_lo))
    def _():
        wbf_ref[...] = wa_ref[...].astype(BF16)

    @pl.when((pl.program_id(1) == 0) & (j >= n_lo))
    def _():
        wbf_ref[0:tn - shift, :] = wa_ref[shift:tn, :].astype(BF16)
        wbf_ref[tn - shift:tn, :] = wb_ref[...].astype(BF16)

    z = lax.dot_general(xn_ref[...], wbf_ref[...], _NT_DIMS, preferred_element_type=F32)
    z_ref[...] = (z + b_ref[...]).astype(z_ref.dtype)


def _inproj(xn, w_in_t, b_all, layer, tm=1024, tn=1024):
    shift = GATE_HI - GATE_LO
    return pl.pallas_call(
        functools.partial(_inproj_kernel, tn=tn),
        out_shape=jax.ShapeDtypeStruct((TOKENS, N_Z), BF16),
        grid=(N_Z // tn, TOKENS // tm),
        in_specs=[
            pl.BlockSpec((tm, D_MODEL), lambda j, i: (i, 0)),
            pl.BlockSpec((None, tn, D_MODEL), lambda j, i: (layer, j, 0)),
            pl.BlockSpec((None, shift, D_MODEL), lambda j, i: (layer, (j + 1) * (tn // shift), 0)),
            pl.BlockSpec((None, 1, tn), lambda j, i: (layer, 0, j)),
        ],
        out_specs=pl.BlockSpec((tm, tn), lambda j, i: (i, j)),
        scratch_shapes=[pltpu.VMEM((tn, D_MODEL), BF16)],
        compiler_params=_cparams(("arbitrary", "arbitrary")),
        name="inproj",
    )(xn, w_in_t, w_in_t, b_all)


def _log_sigmoid(x):
    return -(jnp.maximum(-x, 0.0) + jnp.log1p(jnp.exp(-jnp.abs(x))))


def _cummax_rows(x, rows):
    ridx = lax.broadcasted_iota(jnp.int32, x.shape, 0)
    k = 1
    while k < 8:
        x = jnp.maximum(x, jnp.where(ridx < k, -jnp.inf, pltpu.roll(x, k, 0)))
        k *= 2
    while k < rows:
        pad = jnp.full((k, x.shape[1]), -jnp.inf, x.dtype)
        x = jnp.maximum(x, jnp.concatenate([pad, x[:rows - k]], axis=0))
        k *= 2
    return x


def _mlstm_kernel(q_ref, k_ref, v_ref, o_ref, xn_ref, wgt_ref, bg_ref, mn_ref, out_ref,
                  ct_ref, m_ref, *, L):
    @pl.when(pl.program_id(1) == 0)
    def _():
        ct_ref[...] = jnp.zeros_like(ct_ref)
        m_ref[...] = jnp.zeros_like(m_ref)

    scale = M_QK ** -0.5
    log2e = math.log2(math.e)
    row = lax.broadcasted_iota(jnp.int32, (L, L), 0)
    col = lax.broadcasted_iota(jnp.int32, (L, L), 1)
    causal = row >= col
    zg = lax.dot_general(xn_ref[...], wgt_ref[...].astype(BF16), _NT_DIMS,
                         preferred_element_type=F32)
    z_i = zg + bg_ref[:, 0:LANES]
    z_f = pltpu.roll(zg, LANES - M_HEADS, 1) + bg_ref[:, LANES:2 * LANES]
    lf = _log_sigmoid(z_f)
    b = jnp.dot(causal.astype(F32), lf, precision=lax.Precision.HIGHEST,
                preferred_element_type=F32)
    a = z_i - b
    m_prev = m_ref[0:1, :]
    big_m = jnp.maximum(m_prev, _cummax_rows(a, L))
    a2_t = (a * log2e).T
    m2 = big_m * log2e - math.log2(scale)
    e_inter = jnp.exp(m_prev - big_m) * scale
    e_negm = jnp.exp(-(b + big_m))
    m_last = big_m[L - 1:L, :]
    w_all = jnp.exp(a - m_last)
    decay = jnp.exp(m_prev - m_last)
    m_ref[0:1, :] = b[L - 1:L, :] + m_last
    ones_blk = jnp.ones((L, LANES), BF16)
    tn_dims = (((0,), (0,)), ((), ()))

    for h in range(M_HEADS):
        q = q_ref[:, h * M_QK:(h + 1) * M_QK]
        k = k_ref[:, h * M_QK:(h + 1) * M_QK]
        v = v_ref[:, h * M_V:(h + 1) * M_V]
        m2_b = jnp.broadcast_to(m2[:, h:h + 1], (L, LANES))
        arg = jnp.where(causal, a2_t[h:h + 1, :] - _lane_tile(m2_b, L), -jnp.inf)
        qk = lax.dot_general(q, k, (((1,), (1,)), ((), ())), preferred_element_type=F32)
        s_mat = qk * jnp.exp2(arg)
        ct = ct_ref[h]
        qc = jnp.dot(q, ct.astype(BF16), preferred_element_type=F32)
        inter = e_inter[:, h:h + 1]
        num = (jnp.dot(s_mat.astype(BF16), v, preferred_element_type=F32)
               + jnp.broadcast_to(inter, (L, M_V)) * qc[:, :M_V])
        den = jnp.sum(s_mat, axis=1, keepdims=True) + inter * qc[:, M_V:M_V + 1]
        rden = 1.0 / jnp.maximum(jnp.abs(den), e_negm[:, h:h + 1])
        hh = num * rden
        ms = jnp.mean(hh * hh, axis=-1, keepdims=True)
        hn = (hh * lax.rsqrt(ms + EPS)) * mn_ref[:, h * M_V:(h + 1) * M_V]
        og = o_ref[:, h * M_V:(h + 1) * M_V].astype(F32)
        out_ref[:, h * M_V:(h + 1) * M_V] = (hn * jax.nn.sigmoid(og)).astype(out_ref.dtype)

        kw = (jnp.broadcast_to(w_all[:, h:h + 1], (L, M_QK)) * k.astype(F32)).astype(BF16)
        upd = jnp.concatenate(
            [lax.dot_general(kw, v, tn_dims, preferred_element_type=F32),
             lax.dot_general(kw, ones_blk, tn_dims, preferred_element_type=F32)], axis=1)
        ct_ref[h] = decay[:, h:h + 1] * ct + upd


def _mlstm(z, xn, w_in_t, bg_all, m_norm_all, layer, L=M_CHUNK):
    nc = SEQ // L
    row = lambda b, c: b * nc + c
    return pl.pallas_call(
        functools.partial(_mlstm_kernel, L=L),
        out_shape=jax.ShapeDtypeStruct((TOKENS, M_HEADS * M_V), BF16),
        grid=(BATCH, nc),
        in_specs=[
            pl.BlockSpec((L, 1024), lambda b, c: (row(b, c), Z_Q // 1024)),
            pl.BlockSpec((L, 1024), lambda b, c: (row(b, c), Z_K // 1024)),
            pl.BlockSpec((L, 2048), lambda b, c: (row(b, c), Z_V // 2048)),
            pl.BlockSpec((L, 2048), lambda b, c: (row(b, c), Z_O // 2048)),
            pl.BlockSpec((L, D_MODEL), lambda b, c: (row(b, c), 0)),
            pl.BlockSpec((None, LANES, D_MODEL), lambda b, c: (layer, GATE_LO // LANES, 0)),
            pl.BlockSpec((None, 1, 2 * LANES), lambda b, c: (layer, 0, 0)),
            pl.BlockSpec((None, 1, M_HEADS * M_V), lambda b, c: (layer, 0, 0)),
        ],
        out_specs=pl.BlockSpec((L, M_HEADS * M_V), lambda b, c: (row(b, c), 0)),
        scratch_shapes=[pltpu.VMEM((M_HEADS, M_QK, M_V + LANES), F32),
                        pltpu.VMEM((M_HEADS, LANES), F32)],
        compiler_params=_cparams(("parallel", "arbitrary")),
        name="mlstm",
    )(z, z, z, z, xn, w_in_t, bg_all, m_norm_all)


def _rope(x, cos, sin):
    half = ROPE_DIM // 2
    lane = lax.broadcasted_iota(jnp.int32, x.shape, 1)
    partner = jnp.where(lane < half, pltpu.roll(x, LANES - half, 1), pltpu.roll(x, half, 1))
    return x * cos + partner * sin


def _lane_groups(x, op):
    acc = x[:, 0:LANES]
    for g in range(1, x.shape[1] // LANES):
        acc = op(acc, x[:, g * LANES:(g + 1) * LANES])
    return acc


def _lane_tile(a, width):
    return jnp.concatenate([a] * (width // LANES), axis=1) if width > LANES else a


def _attn_kernel(q_ref, k_ref, v_ref, cos_ref, sin_ref, lq_ref, li_ref, an_ref, out_ref, kr_ref, *, tq):
    qscale = (A_QK ** -0.5) * math.log2(math.e)
    nt = (((1,), (1,)), ((), ()))
    rchunk = 256
    for c in range(SEQ // rchunk):
        r = c * rchunk
        kk = k_ref[r:r + rchunk, :].astype(F32)
        cos = cos_ref[r:r + rchunk, :]
        sin = sin_ref[r:r + rchunk, :]
        for mp in range(2):
            kr_ref[r:r + rchunk, mp * A_QK:(mp + 1) * A_QK] = _rope(
                kk[:, mp * A_QK:(mp + 1) * A_QK], cos, sin).astype(BF16)

    lq = lq_ref[...]
    lam_init = li_ref[0:1, 0:1]
    lam = (jnp.exp(jnp.sum(lq[0:1] * lq[1:2], axis=1, keepdims=True))
           - jnp.exp(jnp.sum(lq[2:3] * lq[3:4], axis=1, keepdims=True)) + lam_init)
    causal = (lax.broadcasted_iota(jnp.int32, (tq, tq), 0)
              >= lax.broadcasted_iota(jnp.int32, (tq, tq), 1))

    for t in range(SEQ // tq):
        r0 = t * tq
        qf = q_ref[r0:r0 + tq, :].astype(F32)
        cos_q = cos_ref[r0:r0 + tq, :]
        sin_q = sin_ref[r0:r0 + tq, :]
        pd, pf, lsum = [], [], []
        for mp in range(2):
            qr = (_rope(qf[:, mp * A_QK:(mp + 1) * A_QK], cos_q, sin_q) * qscale).astype(BF16)
            s_d = lax.dot_general(qr, kr_ref[r0:r0 + tq, mp * A_QK:(mp + 1) * A_QK], nt,
                                  preferred_element_type=F32)
            s_d = jnp.where(causal, s_d, -jnp.inf)
            mc = _lane_groups(s_d, jnp.maximum)
            if t:
                s_f = lax.dot_general(qr, kr_ref[0:r0, mp * A_QK:(mp + 1) * A_QK], nt,
                                      preferred_element_type=F32)
                mc = jnp.maximum(mc, _lane_groups(s_f, jnp.maximum))
            m = jnp.broadcast_to(jnp.max(mc, axis=1, keepdims=True), (tq, LANES))
            p_d = jnp.exp2(s_d - _lane_tile(m, tq))
            ps = _lane_groups(p_d, jnp.add)
            pd.append(p_d)
            if t:
                p_f = jnp.exp2(s_f - _lane_tile(m, r0))
                ps = ps + _lane_groups(p_f, jnp.add)
                pf.append(p_f)
            lsum.append(jnp.sum(ps, axis=1, keepdims=True))
        ratio = jnp.broadcast_to(lam * lsum[0] / lsum[1], (tq, LANES))
        a_d = pd[0] - pd[1] * _lane_tile(ratio, tq)
        o = jnp.dot(a_d.astype(BF16), v_ref[r0:r0 + tq, :], preferred_element_type=F32)
        if t:
            a_f = pf[0] - pf[1] * _lane_tile(ratio, r0)
            o = o + jnp.dot(a_f.astype(BF16), v_ref[0:r0, :], preferred_element_type=F32)
        o = o / lsum[0]
        ms = jnp.mean(o * o, axis=-1, keepdims=True)
        on = ((o * lax.rsqrt(ms + EPS)) * an_ref[...]) * (1.0 - lam_init)
        out_ref[r0:r0 + tq, :] = on.astype(out_ref.dtype)


def _attn(z, cos_t, sin_t, lam_qk, lam_init_arr, a_norm_all, layer, tq=A_BLOCK):
    return pl.pallas_call(
        functools.partial(_attn_kernel, tq=tq),
        out_shape=jax.ShapeDtypeStruct((TOKENS, A_HEADS * A_V), BF16),
        grid=(BATCH, A_HEADS),
        in_specs=[
            pl.BlockSpec((SEQ, 2 * A_QK), lambda b, h: (b, Z_AQ // 256 + h)),
            pl.BlockSpec((SEQ, 2 * A_QK), lambda b, h: (b, Z_AK // 256 + h)),
            pl.BlockSpec((SEQ, A_V), lambda b, h: (b, Z_AV // 256 + h)),
            pl.BlockSpec((SEQ, LANES), lambda b, h: (0, 0)),
            pl.BlockSpec((SEQ, LANES), lambda b, h: (0, 0)),
            pl.BlockSpec((None, 4, A_QK), lambda b, h: (layer, 0, 0)),
            pl.BlockSpec((None, 1, LANES), lambda b, h: (layer, 0, 0)),
            pl.BlockSpec((None, 1, A_V), lambda b, h: (layer, 0, h)),
        ],
        out_specs=pl.BlockSpec((SEQ, A_V), lambda b, h: (b, h)),
        scratch_shapes=[pltpu.VMEM((SEQ, 2 * A_QK), BF16)],
        compiler_params=_cparams(("parallel", "parallel")),
        name="diff_attn",
    )(z, z, z, cos_t, sin_t, lam_qk, lam_init_arr, a_norm_all)


def _mixout_kernel(hm_ref, ha_ref, gm_ref, gd_ref, wm_ref, wd_ref, wo_ref, h_ref, g_ref,
                   out_ref, xn_ref, y_ref, *, tm, nsplit):
    cw = D_MODEL // nsplit
    hm = hm_ref[...]
    ha = ha_ref[...]
    for n in range(nsplit):
        c = slice(n * cw, (n + 1) * cw)
        ym = jnp.dot(hm, wm_ref[:, c], preferred_element_type=F32)
        yd = jnp.dot(ha, wd_ref[:, c], preferred_element_type=F32)
        y = (jax.nn.sigmoid(gm_ref[:, c].astype(F32)) * ym
             + jax.nn.sigmoid(gd_ref[:, c].astype(F32)) * yd)
        y_ref[:, c] = y.astype(y_ref.dtype)
    y = y_ref[...]
    for n in range(nsplit):
        c = slice(n * cw, (n + 1) * cw)
        out_ref[:, c] = h_ref[:, c] + jnp.dot(y, wo_ref[:, c], preferred_element_type=F32)
    _rmsnorm_rows(out_ref, g_ref, xn_ref, tm, chunk=tm)


def _mixout(h, hm, ha, z, wbm_all, wbd_all, wo_all, g_all, layer, tm=256, nsplit=4):
    wspec = pl.BlockSpec((None, D_MODEL, D_MODEL), lambda i: (layer, 0, 0),
                         pipeline_mode=pl.Buffered(1))
    row = pl.BlockSpec((tm, D_MODEL), lambda i: (i, 0))
    return pl.pallas_call(
        functools.partial(_mixout_kernel, tm=tm, nsplit=nsplit),
        out_shape=(jax.ShapeDtypeStruct((TOKENS, D_MODEL), F32),
                   jax.ShapeDtypeStruct((TOKENS, D_MODEL), BF16)),
        grid=(TOKENS // tm,),
        in_specs=[
            row, row,
            pl.BlockSpec((tm, D_MODEL), lambda i: (i, Z_GT // D_MODEL)),
            pl.BlockSpec((tm, D_MODEL), lambda i: (i, Z_GT // D_MODEL + 1)),
            wspec, wspec, wspec,
            row,
            pl.BlockSpec((None, 1, D_MODEL), lambda i: (layer, 0, 0)),
        ],
        out_specs=(row, row),
        scratch_shapes=[pltpu.VMEM((tm, D_MODEL), BF16)],
        compiler_params=_cparams(("parallel",)),
        name="mix_out",
    )(hm, ha, z, z, wbm_all, wbd_all, wo_all, h, g_all)


def _ffn_up_kernel(xn_ref, wg_ref, wv_ref, cw_ref, cb_ref, u_ref, wgb_ref, wvb_ref, carry_ref,
                   *, tm, tn, sub):
    i = pl.program_id(1)

    @pl.when(i == 0)
    def _():
        wgb_ref[...] = wg_ref[...].astype(BF16)
        wvb_ref[...] = wv_ref[...].astype(BF16)

    @pl.when((i % (SEQ // tm)) == 0)
    def _():
        carry_ref[...] = jnp.zeros_like(carry_ref)

    prev = carry_ref[...]
    cw = cw_ref[...]
    cb = cb_ref[...]
    rowi = lax.broadcasted_iota(jnp.int32, (sub, tn), 0)
    is0 = rowi == 0
    is1 = rowi == 1
    for r in range(tm // sub):
        xs = xn_ref[r * sub:(r + 1) * sub, :]
        gate = jnp.dot(xs, wgb_ref[...], preferred_element_type=F32)
        val = jnp.dot(xs, wvb_ref[...], preferred_element_type=F32)
        g1 = jnp.where(is0, prev[7:8, :], pltpu.roll(gate, 1, 0))
        g2 = jnp.where(is0, prev[6:7, :], jnp.where(is1, prev[7:8, :], pltpu.roll(gate, 2, 0)))
        conv = cw[0:1, :] * g2 + cw[1:2, :] * g1 + cw[2:3, :] * gate + cb
        u_ref[r * sub:(r + 1) * sub, :] = ((conv * jax.nn.sigmoid(conv)) * val).astype(u_ref.dtype)
        prev = gate[sub - 8:sub, :]
    carry_ref[...] = prev


def _ffn_up(xn, wup_all, cw_all, cb_all, layer, tm=1024, tn=512, sub=256):
    nf = D_FF // tn
    return pl.pallas_call(
        functools.partial(_ffn_up_kernel, tm=tm, tn=tn, sub=sub),
        out_shape=jax.ShapeDtypeStruct((TOKENS, D_FF), BF16),
        grid=(nf, TOKENS // tm),
        in_specs=[
            pl.BlockSpec((tm, D_MODEL), lambda j, i: (i, 0)),
            pl.BlockSpec((None, D_MODEL, tn), lambda j, i: (layer, 0, j)),
            pl.BlockSpec((None, D_MODEL, tn), lambda j, i: (layer, 0, nf + j)),
            pl.BlockSpec((None, CONV_W, tn), lambda j, i: (layer, 0, j)),
            pl.BlockSpec((None, 1, tn), lambda j, i: (layer, 0, j)),
        ],
        out_specs=pl.BlockSpec((tm, tn), lambda j, i: (i, j)),
        scratch_shapes=[pltpu.VMEM((D_MODEL, tn), BF16),
                        pltpu.VMEM((D_MODEL, tn), BF16),
                        pltpu.VMEM((8, tn), F32)],
        compiler_params=_cparams(("arbitrary", "arbitrary")),
        name="ffn_up",
    )(xn, wup_all, wup_all, cw_all, cb_all)


def _ffn_down_kernel(u_ref, w_ref, h_ref, g_ref, *out_refs, tm, nsplit, last):
    cw = D_MODEL // nsplit
    u = u_ref[...]
    if last:
        (y_ref,) = out_refs
        hn_ref = y_ref
    else:
        hn_ref, y_ref = out_refs
    for n in range(nsplit):
        c = slice(n * cw, (n + 1) * cw)
        hn_ref[:, c] = h_ref[:, c] + jnp.dot(u, w_ref[:, c], preferred_element_type=F32)
    _rmsnorm_rows(hn_ref, g_ref, y_ref, tm, chunk=tm)


def _ffn_down(h, u, w_all, g, layer, last, tm=256, nsplit=4):
    row = pl.BlockSpec((tm, D_MODEL), lambda i: (i, 0))
    if last:
        out_shape = jax.ShapeDtypeStruct((TOKENS, D_MODEL), F32)
        out_specs = row
    else:
        out_shape = (jax.ShapeDtypeStruct((TOKENS, D_MODEL), F32),
                     jax.ShapeDtypeStruct((TOKENS, D_MODEL), BF16))
        out_specs = (row, row)
    return pl.pallas_call(
        functools.partial(_ffn_down_kernel, tm=tm, nsplit=nsplit, last=last),
        out_shape=out_shape,
        grid=(TOKENS // tm,),
        in_specs=[
            pl.BlockSpec((tm, D_FF), lambda i: (i, 0)),
            pl.BlockSpec((None, D_FF, D_MODEL), lambda i: (layer, 0, 0),
                         pipeline_mode=pl.Buffered(1)),
            row,
            pl.BlockSpec((1, D_MODEL), lambda i: (0, 0)),
        ],
        out_specs=out_specs,
        compiler_params=_cparams(("parallel",)),
        name="ffn_down",
    )(u, w_all, h, g)


def _rope_tables():
    half = ROPE_DIM // 2
    inv = ROPE_THETA ** (-jnp.arange(half, dtype=F32) / half)
    ang = jnp.arange(SEQ).astype(F32)[:, None] * inv[None, :]
    cos, sin = jnp.cos(ang), jnp.sin(ang)
    ones = jnp.ones((SEQ, LANES - ROPE_DIM), F32)
    zeros = jnp.zeros((SEQ, LANES - ROPE_DIM), F32)
    return (jnp.concatenate([cos, cos, ones], axis=1),
            jnp.concatenate([-sin, sin, zeros], axis=1))


def kernel(x, norm_mix, w_in, b_in, m_norm, a_norm, lam_qk, w_bm, w_bd, w_o, norm_ffn, w_up, conv_w, conv_b, w_down, norm_final):
    w_in_t = jnp.swapaxes(w_in, 1, 2)
    b_main = jnp.concatenate([b_in[:, :GATE_LO], b_in[:, GATE_HI:]], axis=1)[:, None, :]
    pad = ((0, 0), (0, LANES - M_HEADS))
    b_gate = jnp.concatenate([jnp.pad(b_in[:, GATE_LO:GATE_LO + M_HEADS], pad),
                              jnp.pad(b_in[:, GATE_LO + M_HEADS:GATE_HI], pad)], axis=1)[:, None, :]
    w_bm_b, w_bd_b, w_o_b = w_bm.astype(BF16), w_bd.astype(BF16), w_o.astype(BF16)
    w_down_b = w_down.astype(BF16)
    norm_mix3, norm_ffn3 = norm_mix[:, None, :], norm_ffn[:, None, :]
    m_norm3, a_norm3 = m_norm[:, None, :], a_norm[:, None, :]
    conv_b3 = conv_b[:, None, :]
    lam_init = jnp.asarray([0.8 - 0.6 * math.exp(-0.3 * l) for l in range(DEPTH)], F32)
    lam_init3 = jnp.broadcast_to(lam_init[:, None, None], (DEPTH, 1, LANES))
    cos_t, sin_t = _rope_tables()

    h = x.reshape(TOKENS, D_MODEL)
    xn = _norm(h, norm_mix3, 0)
    for l in range(DEPTH):
        last = l == DEPTH - 1
        z = _inproj(xn, w_in_t, b_main, l)
        hm = _mlstm(z, xn, w_in_t, b_gate, m_norm3, l)
        ha = _attn(z, cos_t, sin_t, lam_qk, lam_init3, a_norm3, l)
        h, xn = _mixout(h, hm, ha, z, w_bm_b, w_bd_b, w_o_b, norm_ffn3, l)
        u = _ffn_up(xn, w_up, conv_w, conv_b3, l)
        if last:
            out = _ffn_down(h, u, w_down_b, norm_final[None, :], l, True)
        else:
            h, xn = _ffn_down(h, u, w_down_b, norm_mix[l + 1][None, :], l, False)
    return out.reshape(BATCH, SEQ, D_MODEL)
```

```python
import functools
import math

import jax
import jax.numpy as jnp
import numpy as np
from jax import lax
from jax.experimental import pallas as pl
from jax.experimental.pallas import tpu as pltpu

F32 = jnp.float32
BF16 = jnp.bfloat16

D_MODEL = 2048
BATCH = 4
SEQ = 2048
DEPTH = 4
TOKENS = BATCH * SEQ

M_HEADS = 8
M_QK = 128
M_V = 256
A_HEADS = 8
A_QK = 128
A_V = 256
ROPE_DIM = A_QK // 4
ROPE_THETA = 500000.0
D_FF = 5632
CONV_W = 3
EPS = 1e-6

_SPLITS = [M_HEADS * M_QK, M_HEADS * M_QK, M_HEADS * M_V, M_HEADS * M_V, M_HEADS, M_HEADS,
           2 * A_HEADS * A_QK, 2 * A_HEADS * A_QK, A_HEADS * A_V, 2 * D_MODEL]
_OFFS = [0] + [int(o) for o in np.cumsum(_SPLITS)]
GATE_LO, GATE_HI = _OFFS[4], _OFFS[6]
Z_Q, Z_K, Z_V, Z_O = 0, 1024, 2048, 4096
Z_AQ, Z_AK, Z_AV, Z_GT = 6144, 8192, 10240, 12288
N_Z = 16384
LANES = 128

VMEM_LIMIT = 56 * 1024 * 1024

M_CHUNK = 256
A_BLOCK = 512


def _cparams(sem):
    return pltpu.CompilerParams(dimension_semantics=sem, vmem_limit_bytes=VMEM_LIMIT)


def _rmsnorm_rows(x_ref, g_ref, out_ref, rows, chunk=256):
    def body(c, carry):
        r = pl.multiple_of(c * chunk, chunk)
        x = x_ref[pl.ds(r, chunk), :]
        ms = jnp.mean(x * x, axis=-1, keepdims=True)
        out_ref[pl.ds(r, chunk), :] = ((x * lax.rsqrt(ms + EPS)) * g_ref[...]).astype(out_ref.dtype)
        return carry
    lax.fori_loop(0, rows // chunk, body, 0)


_NT_DIMS = (((1,), (1,)), ((), ()))


def _norm_kernel(x_ref, g_ref, out_ref, *, tm):
    _rmsnorm_rows(x_ref, g_ref, out_ref, tm)


def _norm(x, g_all, layer, tm=512):
    row = pl.BlockSpec((tm, D_MODEL), lambda i: (i, 0))
    return pl.pallas_call(
        functools.partial(_norm_kernel, tm=tm),
        out_shape=jax.ShapeDtypeStruct((TOKENS, D_MODEL), BF16),
        grid=(TOKENS // tm,),
        in_specs=[row, pl.BlockSpec((None, 1, D_MODEL), lambda i: (layer, 0, 0))],
        out_specs=row,
        compiler_params=_cparams(("parallel",)),
        name="norm_in",
    )(x, g_all)


def _inproj_kernel(xn_ref, wa_ref, wb_ref, b_ref, z_ref, wbf_ref, *, tm, tn, sub):
    j = pl.program_id(0)
    n_lo = GATE_LO // tn
    shift = GATE_HI - GATE_LO

    @pl.when((pl.program_id(1) == 0) & (j < n_lo))
    def _():
        wbf_ref[...] = wa_ref[...].astype(BF16)

    @pl.when((pl.program_id(1) == 0) & (j >= n_lo))
    def _():
        wbf_ref[0:tn - shift, :] = wa_ref[shift:tn, :].astype(BF16)
        wbf_ref[tn - shift:tn, :] = wb_ref[...].astype(BF16)

    for r in range(0, tm, sub):
        z = lax.dot_general(xn_ref[r:r + sub, :], wbf_ref[...], _NT_DIMS, preferred_element_type=F32)
        z_ref[r:r + sub, :] = (z + b_ref[...]).astype(z_ref.dtype)


def _inproj(xn, w_in_t, b_all, layer, tm=2048, tn=1024, sub=1024):
    shift = GATE_HI - GATE_LO
    return pl.pallas_call(
        functools.partial(_inproj_kernel, tm=tm, tn=tn, sub=sub),
        out_shape=jax.ShapeDtypeStruct((TOKENS, N_Z), BF16),
        grid=(N_Z // tn, TOKENS // tm),
        in_specs=[
            pl.BlockSpec((tm, D_MODEL), lambda j, i: (i, 0)),
            pl.BlockSpec((None, tn, D_MODEL), lambda j, i: (layer, j, 0)),
            pl.BlockSpec((None, shift, D_MODEL), lambda j, i: (layer, (j + 1) * (tn // shift), 0)),
            pl.BlockSpec((None, 1, tn), lambda j, i: (layer, 0, j)),
        ],
        out_specs=pl.BlockSpec((tm, tn), lambda j, i: (i, j)),
        scratch_shapes=[pltpu.VMEM((tn, D_MODEL), BF16)],
        compiler_params=_cparams(("arbitrary", "arbitrary")),
        name="inproj",
    )(xn, w_in_t, w_in_t, b_all)


def _log_sigmoid(x):
    return -(jnp.maximum(-x, 0.0) + jnp.log1p(jnp.exp(-jnp.abs(x))))


def _cummax_rows(x, rows):
    ridx = lax.broadcasted_iota(jnp.int32, x.shape, 0)
    k = 1
    while k < 8:
        x = jnp.maximum(x, jnp.where(ridx < k, -jnp.inf, pltpu.roll(x, k, 0)))
        k *= 2
    while k < rows:
        pad = jnp.full((k, x.shape[1]), -jnp.inf, x.dtype)
        x = jnp.maximum(x, jnp.concatenate([pad, x[:rows - k]], axis=0))
        k *= 2
    return x


def _mlstm_kernel(q_ref, k_ref, v_ref, o_ref, xn_ref, wgt_ref, bg_ref, mn_ref, out_ref,
                  ct_ref, m_ref, *, L):
    @pl.when(pl.program_id(1) == 0)
    def _():
        ct_ref[...] = jnp.zeros_like(ct_ref)
        m_ref[...] = jnp.zeros_like(m_ref)

    scale = M_QK ** -0.5
    log2e = math.log2(math.e)
    row = lax.broadcasted_iota(jnp.int32, (L, L), 0)
    col = lax.broadcasted_iota(jnp.int32, (L, L), 1)
    causal = row >= col
    zg = lax.dot_general(xn_ref[...], wgt_ref[...].astype(BF16), _NT_DIMS,
                         preferred_element_type=F32)
    z_i = zg + bg_ref[:, 0:LANES]
    z_f = pltpu.roll(zg, LANES - M_HEADS, 1) + bg_ref[:, LANES:2 * LANES]
    lf = _log_sigmoid(z_f)
    b = jnp.dot(causal.astype(F32), lf, precision=lax.Precision.HIGHEST,
                preferred_element_type=F32)
    a = z_i - b
    m_prev = m_ref[0:1, :]
    big_m = jnp.maximum(m_prev, _cummax_rows(a, L))
    a2_t = (a * log2e).T
    m2 = big_m * log2e - math.log2(scale)
    e_inter = jnp.exp(m_prev - big_m) * scale
    e_negm = jnp.exp(-(b + big_m))
    m_last = big_m[L - 1:L, :]
    w_all = jnp.exp(a - m_last)
    decay = jnp.exp(m_prev - m_last)
    m_ref[0:1, :] = b[L - 1:L, :] + m_last
    ones_blk = jnp.ones((L, LANES), BF16)
    tn_dims = (((0,), (0,)), ((), ()))

    for h in range(M_HEADS):
        q = q_ref[:, h * M_QK:(h + 1) * M_QK]
        k = k_ref[:, h * M_QK:(h + 1) * M_QK]
        v = v_ref[:, h * M_V:(h + 1) * M_V]
        m2_b = jnp.broadcast_to(m2[:, h:h + 1], (L, LANES))
        arg = jnp.where(causal, a2_t[h:h + 1, :] - _lane_tile(m2_b, L), -jnp.inf)
        qk = lax.dot_general(q, k, _NT_DIMS, preferred_element_type=F32)
        s_mat = qk * jnp.exp2(arg)
        ct = ct_ref[h]
        qc = jnp.dot(q, ct.astype(BF16), preferred_element_type=F32)
        inter = e_inter[:, h:h + 1]
        num = (jnp.dot(s_mat.astype(BF16), v, preferred_element_type=F32)
               + jnp.broadcast_to(inter, (L, M_V)) * qc[:, :M_V])
        den = jnp.sum(s_mat, axis=1, keepdims=True) + inter * qc[:, M_V:M_V + 1]
        rden = 1.0 / jnp.maximum(jnp.abs(den), e_negm[:, h:h + 1])
        hh = num * rden
        ms = jnp.mean(hh * hh, axis=-1, keepdims=True)
        hn = (hh * lax.rsqrt(ms + EPS)) * mn_ref[:, h * M_V:(h + 1) * M_V]
        og = o_ref[:, h * M_V:(h + 1) * M_V].astype(F32)
        out_ref[:, h * M_V:(h + 1) * M_V] = (hn * jax.nn.sigmoid(og)).astype(out_ref.dtype)

        kw = (jnp.broadcast_to(w_all[:, h:h + 1], (L, M_QK)) * k.astype(F32)).astype(BF16)
        upd = jnp.concatenate(
            [lax.dot_general(kw, v, tn_dims, preferred_element_type=F32),
             lax.dot_general(kw, ones_blk, tn_dims, preferred_element_type=F32)], axis=1)
        ct_ref[h] = decay[:, h:h + 1] * ct + upd


def _mlstm(z, xn, w_in_t, bg_all, m_norm_all, layer, L=M_CHUNK):
    nc = SEQ // L
    row = lambda b, c: b * nc + c
    return pl.pallas_call(
        functools.partial(_mlstm_kernel, L=L),
        out_shape=jax.ShapeDtypeStruct((TOKENS, M_HEADS * M_V), BF16),
        grid=(BATCH, nc),
        in_specs=[
            pl.BlockSpec((L, 1024), lambda b, c: (row(b, c), Z_Q // 1024)),
            pl.BlockSpec((L, 1024), lambda b, c: (row(b, c), Z_K // 1024)),
            pl.BlockSpec((L, 2048), lambda b, c: (row(b, c), Z_V // 2048)),
            pl.BlockSpec((L, 2048), lambda b, c: (row(b, c), Z_O // 2048)),
            pl.BlockSpec((L, D_MODEL), lambda b, c: (row(b, c), 0)),
            pl.BlockSpec((None, LANES, D_MODEL), lambda b, c: (layer, GATE_LO // LANES, 0)),
            pl.BlockSpec((None, 1, 2 * LANES), lambda b, c: (layer, 0, 0)),
            pl.BlockSpec((None, 1, M_HEADS * M_V), lambda b, c: (layer, 0, 0)),
        ],
        out_specs=pl.BlockSpec((L, M_HEADS * M_V), lambda b, c: (row(b, c), 0)),
        scratch_shapes=[pltpu.VMEM((M_HEADS, M_QK, M_V + LANES), F32),
                        pltpu.VMEM((M_HEADS, LANES), F32)],
        compiler_params=_cparams(("parallel", "arbitrary")),
        name="mlstm",
    )(z, z, z, z, xn, w_in_t, bg_all, m_norm_all)


def _rope(x, cos, sin):
    half = ROPE_DIM // 2
    lane = lax.broadcasted_iota(jnp.int32, x.shape, 1)
    partner = jnp.where(lane < half, pltpu.roll(x, LANES - half, 1), pltpu.roll(x, half, 1))
    return x * cos + partner * sin


def _lane_groups(x, op):
    acc = x[:, 0:LANES]
    for g in range(1, x.shape[1] // LANES):
        acc = op(acc, x[:, g * LANES:(g + 1) * LANES])
    return acc


def _lane_tile(a, width):
    return jnp.concatenate([a] * (width // LANES), axis=1) if width > LANES else a


def _attn_kernel(q_ref, k_ref, v_ref, cos_ref, sin_ref, lq_ref, li_ref, an_ref, out_ref,
                 kr_ref, s_ref, a_ref, *, tq):
    qscale = (A_QK ** -0.5) * math.log2(math.e)
    nt = (((1,), (1,)), ((), ()))
    rchunk = 256

    def rope_keys(r0, rows):
        for r in range(r0, r0 + rows, rchunk):
            kk = k_ref[r:r + rchunk, :].astype(F32)
            cos = cos_ref[r:r + rchunk, :]
            sin = sin_ref[r:r + rchunk, :]
            for mp in range(2):
                kr_ref[r:r + rchunk, mp * A_QK:(mp + 1) * A_QK] = _rope(
                    kk[:, mp * A_QK:(mp + 1) * A_QK], cos, sin).astype(BF16)

    lq = lq_ref[...]
    lam_init = li_ref[0:1, 0:1]
    lam = (jnp.exp(jnp.sum(lq[0:1] * lq[1:2], axis=1, keepdims=True))
           - jnp.exp(jnp.sum(lq[2:3] * lq[3:4], axis=1, keepdims=True)) + lam_init)
    causal = (lax.broadcasted_iota(jnp.int32, (tq, tq), 0)
              >= lax.broadcasted_iota(jnp.int32, (tq, tq), 1))

    nq = SEQ // tq

    def scores(t):
        r0 = t * tq
        rope_keys(r0, tq)
        qf = q_ref[r0:r0 + tq, :].astype(F32)
        cos_q = cos_ref[r0:r0 + tq, :]
        sin_q = sin_ref[r0:r0 + tq, :]
        for mp in range(2):
            qr = (_rope(qf[:, mp * A_QK:(mp + 1) * A_QK], cos_q, sin_q) * qscale).astype(BF16)
            s_ref[t % 2, mp, :, 0:r0 + tq] = lax.dot_general(
                qr, kr_ref[0:r0 + tq, mp * A_QK:(mp + 1) * A_QK], nt, preferred_element_type=F32)

    scores(0)
    for t in range(nq):
        if t + 1 < nq:
            scores(t + 1)
        r0 = t * tq
        buf = t % 2
        lsum = []
        for mp in range(2):
            s_d = jnp.where(causal, s_ref[buf, mp, :, r0:r0 + tq], -jnp.inf)
            mc = _lane_groups(s_d, jnp.maximum)
            if t:
                mc = jnp.maximum(mc, _lane_groups(s_ref[buf, mp, :, 0:r0], jnp.maximum))
            m = jnp.broadcast_to(jnp.max(mc, axis=1, keepdims=True), (tq, LANES))
            p_d = jnp.exp2(s_d - _lane_tile(m, tq))
            ps = _lane_groups(p_d, jnp.add)
            s_ref[buf, mp, :, r0:r0 + tq] = p_d
            if t:
                p_f = jnp.exp2(s_ref[buf, mp, :, 0:r0] - _lane_tile(m, r0))
                ps = ps + _lane_groups(p_f, jnp.add)
                s_ref[buf, mp, :, 0:r0] = p_f
            lsum.append(jnp.sum(ps, axis=1, keepdims=True))
        ratio = jnp.broadcast_to(lam * lsum[0] / lsum[1], (tq, LANES))
        kv = r0 + tq
        a_ref[buf, :, 0:kv] = (s_ref[buf, 0, :, 0:kv]
                               - s_ref[buf, 1, :, 0:kv] * _lane_tile(ratio, kv)).astype(BF16)
        o = jnp.dot(a_ref[buf, :, 0:kv], v_ref[0:kv, :], preferred_element_type=F32)
        o = o / lsum[0]
        ms = jnp.mean(o * o, axis=-1, keepdims=True)
        on = ((o * lax.rsqrt(ms + EPS)) * an_ref[...]) * (1.0 - lam_init)
        out_ref[r0:r0 + tq, :] = on.astype(out_ref.dtype)


def _attn(z, cos_t, sin_t, lam_qk, lam_init_arr, a_norm_all, layer, tq=A_BLOCK):
    return pl.pallas_call(
        functools.partial(_attn_kernel, tq=tq),
        out_shape=jax.ShapeDtypeStruct((TOKENS, A_HEADS * A_V), BF16),
        grid=(BATCH, A_HEADS),
        in_specs=[
            pl.BlockSpec((SEQ, 2 * A_QK), lambda b, h: (b, Z_AQ // 256 + h)),
            pl.BlockSpec((SEQ, 2 * A_QK), lambda b, h: (b, Z_AK // 256 + h)),
            pl.BlockSpec((SEQ, A_V), lambda b, h: (b, Z_AV // 256 + h)),
            pl.BlockSpec((SEQ, LANES), lambda b, h: (0, 0)),
            pl.BlockSpec((SEQ, LANES), lambda b, h: (0, 0)),
            pl.BlockSpec((None, 4, A_QK), lambda b, h: (layer, 0, 0)),
            pl.BlockSpec((None, 1, LANES), lambda b, h: (layer, 0, 0)),
            pl.BlockSpec((None, 1, A_V), lambda b, h: (layer, 0, h)),
        ],
        out_specs=pl.BlockSpec((SEQ, A_V), lambda b, h: (b, h)),
        scratch_shapes=[pltpu.VMEM((SEQ, 2 * A_QK), BF16),
                        pltpu.VMEM((2, 2, tq, SEQ), F32),
                        pltpu.VMEM((2, tq, SEQ), BF16)],
        compiler_params=_cparams(("parallel", "parallel")),
        name="diff_attn",
    )(z, z, z, cos_t, sin_t, lam_qk, lam_init_arr, a_norm_all)


def _mixout_kernel(hm_ref, ha_ref, gm_ref, gd_ref, wm_ref, wd_ref, wo_ref, h_ref, g_ref,
                   out_ref, xn_ref, y_ref, *, tm, nsplit):
    cw = D_MODEL // nsplit
    hm = hm_ref[...]
    ha = ha_ref[...]
    for n in range(nsplit):
        c = slice(n * cw, (n + 1) * cw)
        ym = jnp.dot(hm, wm_ref[:, c], preferred_element_type=F32)
        yd = jnp.dot(ha, wd_ref[:, c], preferred_element_type=F32)
        y = (jax.nn.sigmoid(gm_ref[:, c].astype(F32)) * ym
             + jax.nn.sigmoid(gd_ref[:, c].astype(F32)) * yd)
        y_ref[:, c] = y.astype(y_ref.dtype)
    y = y_ref[...]
    for n in range(nsplit):
        c = slice(n * cw, (n + 1) * cw)
        out_ref[:, c] = h_ref[:, c] + jnp.dot(y, wo_ref[:, c], preferred_element_type=F32)
    _rmsnorm_rows(out_ref, g_ref, xn_ref, tm, chunk=tm)


def _mixout(h, hm, ha, z, wbm_all, wbd_all, wo_all, g_all, layer, tm=256, nsplit=4):
    wspec = pl.BlockSpec((None, D_MODEL, D_MODEL), lambda i: (layer, 0, 0),
                         pipeline_mode=pl.Buffered(1))
    row = pl.BlockSpec((tm, D_MODEL), lambda i: (i, 0))
    return pl.pallas_call(
        functools.partial(_mixout_kernel, tm=tm, nsplit=nsplit),
        out_shape=(jax.ShapeDtypeStruct((TOKENS, D_MODEL), F32),
                   jax.ShapeDtypeStruct((TOKENS, D_MODEL), BF16)),
        grid=(TOKENS // tm,),
        in_specs=[
            row, row,
            pl.BlockSpec((tm, D_MODEL), lambda i: (i, Z_GT // D_MODEL)),
            pl.BlockSpec((tm, D_MODEL), lambda i: (i, Z_GT // D_MODEL + 1)),
            wspec, wspec, wspec,
            row,
            pl.BlockSpec((None, 1, D_MODEL), lambda i: (layer, 0, 0)),
        ],
        out_specs=(row, row),
        scratch_shapes=[pltpu.VMEM((tm, D_MODEL), BF16)],
        compiler_params=_cparams(("parallel",)),
        name="mix_out",
    )(hm, ha, z, z, wbm_all, wbd_all, wo_all, h, g_all)


def _ffn_up_kernel(xn_ref, wg_ref, wv_ref, cw_ref, cb_ref, u_ref, wgb_ref, wvb_ref, carry_ref,
                   *, tm, tn, sub):
    i = pl.program_id(1)

    @pl.when(i == 0)
    def _():
        wgb_ref[...] = wg_ref[...].astype(BF16)
        wvb_ref[...] = wv_ref[...].astype(BF16)

    @pl.when((i % (SEQ // tm)) == 0)
    def _():
        carry_ref[...] = jnp.zeros_like(carry_ref)

    prev = carry_ref[...]
    cw = cw_ref[...]
    cb = cb_ref[...]
    rowi = lax.broadcasted_iota(jnp.int32, (sub, tn), 0)
    is0 = rowi == 0
    is1 = rowi == 1
    for r in range(tm // sub):
        xs = xn_ref[r * sub:(r + 1) * sub, :]
        gate = jnp.dot(xs, wgb_ref[...], preferred_element_type=F32)
        val = jnp.dot(xs, wvb_ref[...], preferred_element_type=F32)
        g1 = jnp.where(is0, prev[7:8, :], pltpu.roll(gate, 1, 0))
        g2 = jnp.where(is0, prev[6:7, :], jnp.where(is1, prev[7:8, :], pltpu.roll(gate, 2, 0)))
        conv = cw[0:1, :] * g2 + cw[1:2, :] * g1 + cw[2:3, :] * gate + cb
        u_ref[r * sub:(r + 1) * sub, :] = ((conv * jax.nn.sigmoid(conv)) * val).astype(u_ref.dtype)
        prev = gate[sub - 8:sub, :]
    carry_ref[...] = prev


def _ffn_up(xn, wup_all, cw_all, cb_all, layer, tm=2048, tn=512, sub=256):
    nf = D_FF // tn
    return pl.pallas_call(
        functools.partial(_ffn_up_kernel, tm=tm, tn=tn, sub=sub),
        out_shape=jax.ShapeDtypeStruct((TOKENS, D_FF), BF16),
        grid=(nf, TOKENS // tm),
        in_specs=[
            pl.BlockSpec((tm, D_MODEL), lambda j, i: (i, 0)),
            pl.BlockSpec((None, D_MODEL, tn), lambda j, i: (layer, 0, j)),
            pl.BlockSpec((None, D_MODEL, tn), lambda j, i: (layer, 0, nf + j)),
            pl.BlockSpec((None, CONV_W, tn), lambda j, i: (layer, 0, j)),
            pl.BlockSpec((None, 1, tn), lambda j, i: (layer, 0, j)),
        ],
        out_specs=pl.BlockSpec((tm, tn), lambda j, i: (i, j)),
        scratch_shapes=[pltpu.VMEM((D_MODEL, tn), BF16),
                        pltpu.VMEM((D_MODEL, tn), BF16),
                        pltpu.VMEM((8, tn), F32)],
        compiler_params=_cparams(("arbitrary", "arbitrary")),
        name="ffn_up",
    )(xn, wup_all, wup_all, cw_all, cb_all)


def _ffn_down_kernel(u_ref, w_ref, h_ref, g_ref, *out_refs, tm, nsplit, last):
    cw = D_MODEL // nsplit
    u = u_ref[...]
    if last:
        (y_ref,) = out_refs
        hn_ref = y_ref
    else:
        hn_ref, y_ref = out_refs
    for n in range(nsplit):
        c = slice(n * cw, (n + 1) * cw)
        hn_ref[:, c] = h_ref[:, c] + jnp.dot(u, w_ref[:, c], preferred_element_type=F32)
    _rmsnorm_rows(hn_ref, g_ref, y_ref, tm, chunk=tm)


def _ffn_down(h, u, w_all, g, layer, last, tm=256, nsplit=4):
    row = pl.BlockSpec((tm, D_MODEL), lambda i: (i, 0))
    if last:
        out_shape = jax.ShapeDtypeStruct((TOKENS, D_MODEL), F32)
        out_specs = row
    else:
        out_shape = (jax.ShapeDtypeStruct((TOKENS, D_MODEL), F32),
                     jax.ShapeDtypeStruct((TOKENS, D_MODEL), BF16))
        out_specs = (row, row)
    return pl.pallas_call(
        functools.partial(_ffn_down_kernel, tm=tm, nsplit=nsplit, last=last),
        out_shape=out_shape,
        grid=(TOKENS // tm,),
        in_specs=[
            pl.BlockSpec((tm, D_FF), lambda i: (i, 0)),
            pl.BlockSpec((None, D_FF, D_MODEL), lambda i: (layer, 0, 0),
                         pipeline_mode=pl.Buffered(1)),
            row,
            pl.BlockSpec((1, D_MODEL), lambda i: (0, 0)),
        ],
        out_specs=out_specs,
        compiler_params=_cparams(("parallel",)),
        name="ffn_down",
    )(u, w_all, h, g)


def _rope_tables():
    half = ROPE_DIM // 2
    inv = ROPE_THETA ** (-jnp.arange(half, dtype=F32) / half)
    ang = jnp.arange(SEQ).astype(F32)[:, None] * inv[None, :]
    cos, sin = jnp.cos(ang), jnp.sin(ang)
    ones = jnp.ones((SEQ, LANES - ROPE_DIM), F32)
    zeros = jnp.zeros((SEQ, LANES - ROPE_DIM), F32)
    return (jnp.concatenate([cos, cos, ones], axis=1),
            jnp.concatenate([-sin, sin, zeros], axis=1))


def kernel(x, norm_mix, w_in, b_in, m_norm, a_norm, lam_qk, w_bm, w_bd, w_o, norm_ffn, w_up, conv_w, conv_b, w_down, norm_final):
    w_in_t = jnp.swapaxes(w_in, 1, 2)
    b_main = jnp.concatenate([b_in[:, :GATE_LO], b_in[:, GATE_HI:]], axis=1)[:, None, :]
    pad = ((0, 0), (0, LANES - M_HEADS))
    b_gate = jnp.concatenate([jnp.pad(b_in[:, GATE_LO:GATE_LO + M_HEADS], pad),
                              jnp.pad(b_in[:, GATE_LO + M_HEADS:GATE_HI], pad)], axis=1)[:, None, :]
    w_bm_b, w_bd_b, w_o_b = w_bm.astype(BF16), w_bd.astype(BF16), w_o.astype(BF16)
    w_down_b = w_down.astype(BF16)
    norm_mix3, norm_ffn3 = norm_mix[:, None, :], norm_ffn[:, None, :]
    m_norm3, a_norm3 = m_norm[:, None, :], a_norm[:, None, :]
    conv_b3 = conv_b[:, None, :]
    lam_init = jnp.asarray([0.8 - 0.6 * math.exp(-0.3 * l) for l in range(DEPTH)], F32)
    lam_init3 = jnp.broadcast_to(lam_init[:, None, None], (DEPTH, 1, LANES))
    cos_t, sin_t = _rope_tables()

    h = x.reshape(TOKENS, D_MODEL)
    xn = _norm(h, norm_mix3, 0)
    for l in range(DEPTH):
        last = l == DEPTH - 1
        z = _inproj(xn, w_in_t, b_main, l)
        hm = _mlstm(z, xn, w_in_t, b_gate, m_norm3, l)
        ha = _attn(z, cos_t, sin_t, lam_qk, lam_init3, a_norm3, l)
        h, xn = _mixout(h, hm, ha, z, w_bm_b, w_bd_b, w_o_b, norm_ffn3, l)
        u = _ffn_up(xn, w_up, conv_w, conv_b3, l)
        if last:
            out = _ffn_down(h, u, w_down_b, norm_final[None, :], l, True)
        else:
            h, xn = _ffn_down(h, u, w_down_b, norm_mix[l + 1][None, :], l, False)
    return out.reshape(BATCH, SEQ, D_MODEL)
```

```python
import functools
import math

import jax
import jax.numpy as jnp
import numpy as np
from jax import lax
from jax.experimental import pallas as pl
from jax.experimental.pallas import tpu as pltpu

F32 = jnp.float32
BF16 = jnp.bfloat16

D_MODEL = 2048
BATCH = 4
SEQ = 2048
DEPTH = 4
TOKENS = BATCH * SEQ

M_HEADS = 8
M_QK = 128
M_V = 256
A_HEADS = 8
A_QK = 128
A_V = 256
ROPE_DIM = A_QK // 4
ROPE_THETA = 500000.0
D_FF = 5632
CONV_W = 3
EPS = 1e-6

_SPLITS = [M_HEADS * M_QK, M_HEADS * M_QK, M_HEADS * M_V, M_HEADS * M_V, M_HEADS, M_HEADS,
           2 * A_HEADS * A_QK, 2 * A_HEADS * A_QK, A_HEADS * A_V, 2 * D_MODEL]
_OFFS = [0] + [int(o) for o in np.cumsum(_SPLITS)]
GATE_LO, GATE_HI = _OFFS[4], _OFFS[6]
Z_Q, Z_K, Z_V, Z_O = 0, 1024, 2048, 4096
Z_AQ, Z_AK, Z_AV, Z_GT = 6144, 8192, 10240, 12288
N_Z = 16384
LANES = 128

VMEM_LIMIT = 56 * 1024 * 1024

M_CHUNK = 256
A_BLOCK = 512


def _cparams(sem):
    return pltpu.CompilerParams(dimension_semantics=sem, vmem_limit_bytes=VMEM_LIMIT)


def _rmsnorm_rows(x_ref, g_ref, out_ref, rows, chunk=256):
    def body(c, carry):
        r = pl.multiple_of(c * chunk, chunk)
        x = x_ref[pl.ds(r, chunk), :]
        ms = jnp.mean(x * x, axis=-1, keepdims=True)
        out_ref[pl.ds(r, chunk), :] = ((x * lax.rsqrt(ms + EPS)) * g_ref[...]).astype(out_ref.dtype)
        return carry
    lax.fori_loop(0, rows // chunk, body, 0)


_NT_DIMS = (((1,), (1,)), ((), ()))


def _norm_kernel(x_ref, g_ref, out_ref, *, tm):
    _rmsnorm_rows(x_ref, g_ref, out_ref, tm)


def _norm(x, g_all, layer, tm=512):
    row = pl.BlockSpec((tm, D_MODEL), lambda i: (i, 0))
    return pl.pallas_call(
        functools.partial(_norm_kernel, tm=tm),
        out_shape=jax.ShapeDtypeStruct((TOKENS, D_MODEL), BF16),
        grid=(TOKENS // tm,),
        in_specs=[row, pl.BlockSpec((None, 1, D_MODEL), lambda i: (layer, 0, 0))],
        out_specs=row,
        compiler_params=_cparams(("parallel",)),
        name="norm_in",
    )(x, g_all)


def _inproj_kernel(xn_ref, wa_ref, wb_ref, b_ref, c0_ref, c1_ref, c2_ref,
                   z_ref, o0_ref, o1_ref, o2_ref, wbf_ref, *, tm, tn, sub):
    j = pl.program_id(0)
    n_lo = GATE_LO // tn
    shift = GATE_HI - GATE_LO

    @pl.when((pl.program_id(1) == 0) & (j < n_lo))
    def _():
        wbf_ref[...] = wa_ref[...].astype(BF16)

    @pl.when((pl.program_id(1) == 0) & (j >= n_lo))
    def _():
        wbf_ref[0:tn - shift, :] = wa_ref[shift:tn, :].astype(BF16)
        wbf_ref[tn - shift:tn, :] = wb_ref[...].astype(BF16)

    for src, dst in ((c0_ref, o0_ref), (c1_ref, o1_ref), (c2_ref, o2_ref)):
        dst[...] = src[...].astype(BF16)

    for r in range(0, tm, sub):
        z = lax.dot_general(xn_ref[r:r + sub, :], wbf_ref[...], _NT_DIMS, preferred_element_type=F32)
        z_ref[r:r + sub, :] = (z + b_ref[...]).astype(z_ref.dtype)


def _inproj(xn, w_in_t, b_all, w_bm, w_bd, w_o, layer, tm=2048, tn=1024, sub=1024):
    shift = GATE_HI - GATE_LO
    nm = TOKENS // tm
    slab = D_MODEL // ((N_Z // tn) * nm)
    cast_in = pl.BlockSpec((None, slab, D_MODEL), lambda j, i: (layer, j * nm + i, 0))
    cast_out = pl.BlockSpec((slab, D_MODEL), lambda j, i: (j * nm + i, 0))
    wshape = jax.ShapeDtypeStruct((D_MODEL, D_MODEL), BF16)
    return pl.pallas_call(
        functools.partial(_inproj_kernel, tm=tm, tn=tn, sub=sub),
        out_shape=(jax.ShapeDtypeStruct((TOKENS, N_Z), BF16), wshape, wshape, wshape),
        grid=(N_Z // tn, nm),
        in_specs=[
            pl.BlockSpec((tm, D_MODEL), lambda j, i: (i, 0)),
            pl.BlockSpec((None, tn, D_MODEL), lambda j, i: (layer, j, 0)),
            pl.BlockSpec((None, shift, D_MODEL), lambda j, i: (layer, (j + 1) * (tn // shift), 0)),
            pl.BlockSpec((None, 1, tn), lambda j, i: (layer, 0, j)),
            cast_in, cast_in, cast_in,
        ],
        out_specs=(pl.BlockSpec((tm, tn), lambda j, i: (i, j)), cast_out, cast_out, cast_out),
        scratch_shapes=[pltpu.VMEM((tn, D_MODEL), BF16)],
        compiler_params=_cparams(("arbitrary", "arbitrary")),
        name="inproj",
    )(xn, w_in_t, w_in_t, b_all, w_bm, w_bd, w_o)


def _log_sigmoid(x):
    return -(jnp.maximum(-x, 0.0) + jnp.log1p(jnp.exp(-jnp.abs(x))))


def _cummax_rows(x, rows):
    ridx = lax.broadcasted_iota(jnp.int32, x.shape, 0)
    k = 1
    while k < 8:
        x = jnp.maximum(x, jnp.where(ridx < k, -jnp.inf, pltpu.roll(x, k, 0)))
        k *= 2
    while k < rows:
        pad = jnp.full((k, x.shape[1]), -jnp.inf, x.dtype)
        x = jnp.maximum(x, jnp.concatenate([pad, x[:rows - k]], axis=0))
        k *= 2
    return x


def _mlstm_kernel(q_ref, k_ref, v_ref, o_ref, xn_ref, wgt_ref, bg_ref, mn_ref, out_ref,
                  ct_ref, m_ref, *, L):
    @pl.when(pl.program_id(1) == 0)
    def _():
        ct_ref[...] = jnp.zeros_like(ct_ref)
        m_ref[...] = jnp.zeros_like(m_ref)

    scale = M_QK ** -0.5
    log2e = math.log2(math.e)
    row = lax.broadcasted_iota(jnp.int32, (L, L), 0)
    col = lax.broadcasted_iota(jnp.int32, (L, L), 1)
    causal = row >= col
    zg = lax.dot_general(xn_ref[...], wgt_ref[...].astype(BF16), _NT_DIMS,
                         preferred_element_type=F32)
    z_i = zg + bg_ref[:, 0:LANES]
    z_f = pltpu.roll(zg, LANES - M_HEADS, 1) + bg_ref[:, LANES:2 * LANES]
    lf = _log_sigmoid(z_f)
    b = jnp.dot(causal.astype(F32), lf, precision=lax.Precision.HIGHEST,
                preferred_element_type=F32)
    a = z_i - b
    m_prev = m_ref[0:1, :]
    big_m = jnp.maximum(m_prev, _cummax_rows(a, L))
    a2_t = (a * log2e).T
    m2 = big_m * log2e - math.log2(scale)
    e_inter = jnp.exp(m_prev - big_m) * scale
    e_negm = jnp.exp(-(b + big_m))
    m_last = big_m[L - 1:L, :]
    w_all = jnp.exp(a - m_last)
    decay = jnp.exp(m_prev - m_last)
    m_ref[0:1, :] = b[L - 1:L, :] + m_last
    ones_blk = jnp.ones((L, LANES), BF16)
    tn_dims = (((0,), (0,)), ((), ()))

    for h in range(M_HEADS):
        q = q_ref[:, h * M_QK:(h + 1) * M_QK]
        k = k_ref[:, h * M_QK:(h + 1) * M_QK]
        v = v_ref[:, h * M_V:(h + 1) * M_V]
        m2_b = jnp.broadcast_to(m2[:, h:h + 1], (L, LANES))
        arg = jnp.where(causal, a2_t[h:h + 1, :] - _lane_tile(m2_b, L), -jnp.inf)
        qk = lax.dot_general(q, k, _NT_DIMS, preferred_element_type=F32)
        s_mat = qk * jnp.exp2(arg)
        ct = ct_ref[h]
        qc = jnp.dot(q, ct.astype(BF16), preferred_element_type=F32)
        inter = e_inter[:, h:h + 1]
        num = (jnp.dot(s_mat.astype(BF16), v, preferred_element_type=F32)
               + jnp.broadcast_to(inter, (L, M_V)) * qc[:, :M_V])
        den = jnp.sum(s_mat, axis=1, keepdims=True) + inter * qc[:, M_V:M_V + 1]
        rden = 1.0 / jnp.maximum(jnp.abs(den), e_negm[:, h:h + 1])
        hh = num * rden
        ms = jnp.mean(hh * hh, axis=-1, keepdims=True)
        hn = (hh * lax.rsqrt(ms + EPS)) * mn_ref[:, h * M_V:(h + 1) * M_V]
        og = o_ref[:, h * M_V:(h + 1) * M_V].astype(F32)
        out_ref[:, h * M_V:(h + 1) * M_V] = (hn * jax.nn.sigmoid(og)).astype(out_ref.dtype)

        kw = (jnp.broadcast_to(w_all[:, h:h + 1], (L, M_QK)) * k.astype(F32)).astype(BF16)
        upd = jnp.concatenate(
            [lax.dot_general(kw, v, tn_dims, preferred_element_type=F32),
             lax.dot_general(kw, ones_blk, tn_dims, preferred_element_type=F32)], axis=1)
        ct_ref[h] = decay[:, h:h + 1] * ct + upd


def _mlstm(z, xn, w_in_t, bg_all, m_norm_all, layer, L=M_CHUNK):
    nc = SEQ // L
    row = lambda b, c: b * nc + c
    return pl.pallas_call(
        functools.partial(_mlstm_kernel, L=L),
        out_shape=jax.ShapeDtypeStruct((TOKENS, M_HEADS * M_V), BF16),
        grid=(BATCH, nc),
        in_specs=[
            pl.BlockSpec((L, 1024), lambda b, c: (row(b, c), Z_Q // 1024)),
            pl.BlockSpec((L, 1024), lambda b, c: (row(b, c), Z_K // 1024)),
            pl.BlockSpec((L, 2048), lambda b, c: (row(b, c), Z_V // 2048)),
            pl.BlockSpec((L, 2048), lambda b, c: (row(b, c), Z_O // 2048)),
            pl.BlockSpec((L, D_MODEL), lambda b, c: (row(b, c), 0)),
            pl.BlockSpec((None, LANES, D_MODEL), lambda b, c: (layer, GATE_LO // LANES, 0)),
            pl.BlockSpec((None, 1, 2 * LANES), lambda b, c: (layer, 0, 0)),
            pl.BlockSpec((None, 1, M_HEADS * M_V), lambda b, c: (layer, 0, 0)),
        ],
        out_specs=pl.BlockSpec((L, M_HEADS * M_V), lambda b, c: (row(b, c), 0)),
        scratch_shapes=[pltpu.VMEM((M_HEADS, M_QK, M_V + LANES), F32),
                        pltpu.VMEM((M_HEADS, LANES), F32)],
        compiler_params=_cparams(("parallel", "arbitrary")),
        name="mlstm",
    )(z, z, z, z, xn, w_in_t, bg_all, m_norm_all)


def _rope(x, cos, sin):
    half = ROPE_DIM // 2
    lane = lax.broadcasted_iota(jnp.int32, x.shape, 1)
    partner = jnp.where(lane < half, pltpu.roll(x, LANES - half, 1), pltpu.roll(x, half, 1))
    return x * cos + partner * sin


def _lane_groups(x, op):
    acc = x[:, 0:LANES]
    for g in range(1, x.shape[1] // LANES):
        acc = op(acc, x[:, g * LANES:(g + 1) * LANES])
    return acc


def _lane_tile(a, width):
    return jnp.concatenate([a] * (width // LANES), axis=1) if width > LANES else a


def _attn_kernel(q_ref, k_ref, v_ref, cos_ref, sin_ref, lq_ref, li_ref, an_ref, out_ref,
                 kr_ref, s_ref, a_ref, *, tq):
    qscale = (A_QK ** -0.5) * math.log2(math.e)
    nt = (((1,), (1,)), ((), ()))
    rchunk = 256

    def rope_keys(r0, rows):
        for r in range(r0, r0 + rows, rchunk):
            kk = k_ref[r:r + rchunk, :].astype(F32)
            cos = cos_ref[r:r + rchunk, :]
            sin = sin_ref[r:r + rchunk, :]
            for mp in range(2):
                kr_ref[r:r + rchunk, mp * A_QK:(mp + 1) * A_QK] = _rope(
                    kk[:, mp * A_QK:(mp + 1) * A_QK], cos, sin).astype(BF16)

    lq = lq_ref[...]
    lam_init = li_ref[0:1, 0:1]
    lam = (jnp.exp(jnp.sum(lq[0:1] * lq[1:2], axis=1, keepdims=True))
           - jnp.exp(jnp.sum(lq[2:3] * lq[3:4], axis=1, keepdims=True)) + lam_init)
    causal = (lax.broadcasted_iota(jnp.int32, (tq, tq), 0)
              >= lax.broadcasted_iota(jnp.int32, (tq, tq), 1))

    nq = SEQ // tq

    def scores(t):
        r0 = t * tq
        rope_keys(r0, tq)
        qf = q_ref[r0:r0 + tq, :].astype(F32)
        cos_q = cos_ref[r0:r0 + tq, :]
        sin_q = sin_ref[r0:r0 + tq, :]
        for mp in range(2):
            qr = (_rope(qf[:, mp * A_QK:(mp + 1) * A_QK], cos_q, sin_q) * qscale).astype(BF16)
            s_ref[t % 2, mp, :, 0:r0 + tq] = lax.dot_general(
                qr, kr_ref[0:r0 + tq, mp * A_QK:(mp + 1) * A_QK], nt, preferred_element_type=F32)

    scores(0)
    for t in range(nq):
        if t + 1 < nq:
            scores(t + 1)
        r0 = t * tq
        buf = t % 2
        lsum = []
        for mp in range(2):
            s_d = jnp.where(causal, s_ref[buf, mp, :, r0:r0 + tq], -jnp.inf)
            mc = _lane_groups(s_d, jnp.maximum)
            if t:
                mc = jnp.maximum(mc, _lane_groups(s_ref[buf, mp, :, 0:r0], jnp.maximum))
            m = jnp.broadcast_to(jnp.max(mc, axis=1, keepdims=True), (tq, LANES))
            p_d = jnp.exp2(s_d - _lane_tile(m, tq))
            ps = _lane_groups(p_d, jnp.add)
            s_ref[buf, mp, :, r0:r0 + tq] = p_d
            if t:
                p_f = jnp.exp2(s_ref[buf, mp, :, 0:r0] - _lane_tile(m, r0))
                ps = ps + _lane_groups(p_f, jnp.add)
                s_ref[buf, mp, :, 0:r0] = p_f
            lsum.append(jnp.sum(ps, axis=1, keepdims=True))
        ratio = jnp.broadcast_to(lam * lsum[0] / lsum[1], (tq, LANES))
        kv = r0 + tq
        a_ref[buf, :, 0:kv] = (s_ref[buf, 0, :, 0:kv]
                               - s_ref[buf, 1, :, 0:kv] * _lane_tile(ratio, kv)).astype(BF16)
        o = jnp.dot(a_ref[buf, :, 0:kv], v_ref[0:kv, :], preferred_element_type=F32)
        o = o / lsum[0]
        ms = jnp.mean(o * o, axis=-1, keepdims=True)
        on = ((o * lax.rsqrt(ms + EPS)) * an_ref[...]) * (1.0 - lam_init)
        out_ref[r0:r0 + tq, :] = on.astype(out_ref.dtype)


def _attn(z, cos_t, sin_t, lam_qk, lam_init_arr, a_norm_all, layer, tq=A_BLOCK):
    return pl.pallas_call(
        functools.partial(_attn_kernel, tq=tq),
        out_shape=jax.ShapeDtypeStruct((TOKENS, A_HEADS * A_V), BF16),
        grid=(BATCH, A_HEADS),
        in_specs=[
            pl.BlockSpec((SEQ, 2 * A_QK), lambda b, h: (b, Z_AQ // 256 + h)),
            pl.BlockSpec((SEQ, 2 * A_QK), lambda b, h: (b, Z_AK // 256 + h)),
            pl.BlockSpec((SEQ, A_V), lambda b, h: (b, Z_AV // 256 + h)),
            pl.BlockSpec((SEQ, LANES), lambda b, h: (0, 0)),
            pl.BlockSpec((SEQ, LANES), lambda b, h: (0, 0)),
            pl.BlockSpec((None, 4, A_QK), lambda b, h: (layer, 0, 0)),
            pl.BlockSpec((None, 1, LANES), lambda b, h: (layer, 0, 0)),
            pl.BlockSpec((None, 1, A_V), lambda b, h: (layer, 0, h)),
        ],
        out_specs=pl.BlockSpec((SEQ, A_V), lambda b, h: (b, h)),
        scratch_shapes=[pltpu.VMEM((SEQ, 2 * A_QK), BF16),
                        pltpu.VMEM((2, 2, tq, SEQ), F32),
                        pltpu.VMEM((2, tq, SEQ), BF16)],
        compiler_params=_cparams(("parallel", "parallel")),
        name="diff_attn",
    )(z, z, z, cos_t, sin_t, lam_qk, lam_init_arr, a_norm_all)


def _mixout_kernel(hm_ref, ha_ref, gm_ref, gd_ref, wm_ref, wd_ref, wo_ref, h_ref, g_ref,
                   out_ref, xn_ref, y_ref, *, tm, nsplit):
    cw = D_MODEL // nsplit
    hm = hm_ref[...]
    ha = ha_ref[...]
    for n in range(nsplit):
        c = slice(n * cw, (n + 1) * cw)
        ym = jnp.dot(hm, wm_ref[:, c], preferred_element_type=F32)
        yd = jnp.dot(ha, wd_ref[:, c], preferred_element_type=F32)
        y = (jax.nn.sigmoid(gm_ref[:, c].astype(F32)) * ym
             + jax.nn.sigmoid(gd_ref[:, c].astype(F32)) * yd)
        y_ref[:, c] = y.astype(y_ref.dtype)
    y = y_ref[...]
    for n in range(nsplit):
        c = slice(n * cw, (n + 1) * cw)
        out_ref[:, c] = h_ref[:, c] + jnp.dot(y, wo_ref[:, c], preferred_element_type=F32)
    _rmsnorm_rows(out_ref, g_ref, xn_ref, tm, chunk=tm)


def _mixout(h, hm, ha, z, wbm, wbd, wo, g_all, layer, tm=256, nsplit=4):
    wspec = pl.BlockSpec((D_MODEL, D_MODEL), lambda i: (0, 0), pipeline_mode=pl.Buffered(1))
    row = pl.BlockSpec((tm, D_MODEL), lambda i: (i, 0))
    return pl.pallas_call(
        functools.partial(_mixout_kernel, tm=tm, nsplit=nsplit),
        out_shape=(jax.ShapeDtypeStruct((TOKENS, D_MODEL), F32),
                   jax.ShapeDtypeStruct((TOKENS, D_MODEL), BF16)),
        grid=(TOKENS // tm,),
        in_specs=[
            row, row,
            pl.BlockSpec((tm, D_MODEL), lambda i: (i, Z_GT // D_MODEL)),
            pl.BlockSpec((tm, D_MODEL), lambda i: (i, Z_GT // D_MODEL + 1)),
            wspec, wspec, wspec,
            row,
            pl.BlockSpec((None, 1, D_MODEL), lambda i: (layer, 0, 0)),
        ],
        out_specs=(row, row),
        scratch_shapes=[pltpu.VMEM((tm, D_MODEL), BF16)],
        compiler_params=_cparams(("parallel",)),
        name="mix_out",
    )(hm, ha, z, z, wbm, wbd, wo, h, g_all)


def _ffn_up_kernel(xn_ref, wg_ref, wv_ref, cw_ref, cb_ref, wd_ref, u_ref, wdb_ref,
                   wgb_ref, wvb_ref, carry_ref, *, tm, tn, sub):
    i = pl.program_id(1)
    wdb_ref[...] = wd_ref[...].astype(BF16)

    @pl.when(i == 0)
    def _():
        wgb_ref[...] = wg_ref[...].astype(BF16)
        wvb_ref[...] = wv_ref[...].astype(BF16)

    @pl.when((i % (SEQ // tm)) == 0)
    def _():
        carry_ref[...] = jnp.zeros_like(carry_ref)

    prev = carry_ref[...]
    cw = cw_ref[...]
    cb = cb_ref[...]
    rowi = lax.broadcasted_iota(jnp.int32, (sub, tn), 0)
    is0 = rowi == 0
    is1 = rowi == 1
    for r in range(tm // sub):
        xs = xn_ref[r * sub:(r + 1) * sub, :]
        gate = jnp.dot(xs, wgb_ref[...], preferred_element_type=F32)
        val = jnp.dot(xs, wvb_ref[...], preferred_element_type=F32)
        g1 = jnp.where(is0, prev[7:8, :], pltpu.roll(gate, 1, 0))
        g2 = jnp.where(is0, prev[6:7, :], jnp.where(is1, prev[7:8, :], pltpu.roll(gate, 2, 0)))
        conv = cw[0:1, :] * g2 + cw[1:2, :] * g1 + cw[2:3, :] * gate + cb
        u_ref[r * sub:(r + 1) * sub, :] = ((conv * jax.nn.sigmoid(conv)) * val).astype(u_ref.dtype)
        prev = gate[sub - 8:sub, :]
    carry_ref[...] = prev


def _ffn_up(xn, wup_all, cw_all, cb_all, wdown_all, layer, tm=2048, tn=512, sub=1024):
    nf = D_FF // tn
    nm = TOKENS // tm
    slab = D_FF // (nf * nm)
    return pl.pallas_call(
        functools.partial(_ffn_up_kernel, tm=tm, tn=tn, sub=sub),
        out_shape=(jax.ShapeDtypeStruct((TOKENS, D_FF), BF16),
                   jax.ShapeDtypeStruct((D_FF, D_MODEL), BF16)),
        grid=(nf, nm),
        in_specs=[
            pl.BlockSpec((tm, D_MODEL), lambda j, i: (i, 0)),
            pl.BlockSpec((None, D_MODEL, tn), lambda j, i: (layer, 0, j)),
            pl.BlockSpec((None, D_MODEL, tn), lambda j, i: (layer, 0, nf + j)),
            pl.BlockSpec((None, CONV_W, tn), lambda j, i: (layer, 0, j)),
            pl.BlockSpec((None, 1, tn), lambda j, i: (layer, 0, j)),
            pl.BlockSpec((None, slab, D_MODEL), lambda j, i: (layer, j * nm + i, 0)),
        ],
        out_specs=(pl.BlockSpec((tm, tn), lambda j, i: (i, j)),
                   pl.BlockSpec((slab, D_MODEL), lambda j, i: (j * nm + i, 0))),
        scratch_shapes=[pltpu.VMEM((D_MODEL, tn), BF16),
                        pltpu.VMEM((D_MODEL, tn), BF16),
                        pltpu.VMEM((8, tn), F32)],
        compiler_params=_cparams(("arbitrary", "arbitrary")),
        name="ffn_up",
    )(xn, wup_all, wup_all, cw_all, cb_all, wdown_all)


def _ffn_down_kernel(u_ref, w_ref, h_ref, g_ref, *out_refs, tm, nsplit, last):
    cw = D_MODEL // nsplit
    u = u_ref[...]
    if last:
        (y_ref,) = out_refs
        hn_ref = y_ref
    else:
        hn_ref, y_ref = out_refs
    for n in range(nsplit):
        c = slice(n * cw, (n + 1) * cw)
        hn_ref[:, c] = h_ref[:, c] + jnp.dot(u, w_ref[:, c], preferred_element_type=F32)
    _rmsnorm_rows(hn_ref, g_ref, y_ref, tm, chunk=tm)


def _ffn_down(h, u, w, g, last, tm=256, nsplit=4):
    row = pl.BlockSpec((tm, D_MODEL), lambda i: (i, 0))
    if last:
        out_shape = jax.ShapeDtypeStruct((TOKENS, D_MODEL), F32)
        out_specs = row
    else:
        out_shape = (jax.ShapeDtypeStruct((TOKENS, D_MODEL), F32),
                     jax.ShapeDtypeStruct((TOKENS, D_MODEL), BF16))
        out_specs = (row, row)
    return pl.pallas_call(
        functools.partial(_ffn_down_kernel, tm=tm, nsplit=nsplit, last=last),
        out_shape=out_shape,
        grid=(TOKENS // tm,),
        in_specs=[
            pl.BlockSpec((tm, D_FF), lambda i: (i, 0)),
            pl.BlockSpec((D_FF, D_MODEL), lambda i: (0, 0), pipeline_mode=pl.Buffered(1)),
            row,
            pl.BlockSpec((1, D_MODEL), lambda i: (0, 0)),
        ],
        out_specs=out_specs,
        compiler_params=_cparams(("parallel",)),
        name="ffn_down",
    )(u, w, h, g)


def _rope_tables():
    half = ROPE_DIM // 2
    inv = ROPE_THETA ** (-jnp.arange(half, dtype=F32) / half)
    ang = jnp.arange(SEQ).astype(F32)[:, None] * inv[None, :]
    cos, sin = jnp.cos(ang), jnp.sin(ang)
    ones = jnp.ones((SEQ, LANES - ROPE_DIM), F32)
    zeros = jnp.zeros((SEQ, LANES - ROPE_DIM), F32)
    return (jnp.concatenate([cos, cos, ones], axis=1),
            jnp.concatenate([-sin, sin, zeros], axis=1))


def kernel(x, norm_mix, w_in, b_in, m_norm, a_norm, lam_qk, w_bm, w_bd, w_o, norm_ffn, w_up, conv_w, conv_b, w_down, norm_final):
    w_in_t = jnp.swapaxes(w_in, 1, 2)
    b_main = jnp.concatenate([b_in[:, :GATE_LO], b_in[:, GATE_HI:]], axis=1)[:, None, :]
    pad = ((0, 0), (0, LANES - M_HEADS))
    b_gate = jnp.concatenate([jnp.pad(b_in[:, GATE_LO:GATE_LO + M_HEADS], pad),
                              jnp.pad(b_in[:, GATE_LO + M_HEADS:GATE_HI], pad)], axis=1)[:, None, :]
    norm_mix3, norm_ffn3 = norm_mix[:, None, :], norm_ffn[:, None, :]
    m_norm3, a_norm3 = m_norm[:, None, :], a_norm[:, None, :]
    conv_b3 = conv_b[:, None, :]
    lam_init = jnp.asarray([0.8 - 0.6 * math.exp(-0.3 * l) for l in range(DEPTH)], F32)
    lam_init3 = jnp.broadcast_to(lam_init[:, None, None], (DEPTH, 1, LANES))
    cos_t, sin_t = _rope_tables()

    h = x.reshape(TOKENS, D_MODEL)
    xn = _norm(h, norm_mix3, 0)
    for l in range(DEPTH):
        last = l == DEPTH - 1
        z, w_bm_b, w_bd_b, w_o_b = _inproj(xn, w_in_t, b_main, w_bm, w_bd, w_o, l)
        hm = _mlstm(z, xn, w_in_t, b_gate, m_norm3, l)
        ha = _attn(z, cos_t, sin_t, lam_qk, lam_init3, a_norm3, l)
        h, xn = _mixout(h, hm, ha, z, w_bm_b, w_bd_b, w_o_b, norm_ffn3, l)
        u, w_down_b = _ffn_up(xn, w_up, conv_w, conv_b3, w_down, l)
        if last:
            out = _ffn_down(h, u, w_down_b, norm_final[None, :], True)
        else:
            h, xn = _ffn_down(h, u, w_down_b, norm_mix[l + 1][None, :], False)
    return out.reshape(BATCH, SEQ, D_MODEL)
```

```python
import functools
import math

import jax
import jax.numpy as jnp
import numpy as np
from jax import lax
from jax.experimental import pallas as pl
from jax.experimental.pallas import tpu as pltpu

F32 = jnp.float32
BF16 = jnp.bfloat16

D_MODEL = 2048
BATCH = 4
SEQ = 2048
DEPTH = 4
TOKENS = BATCH * SEQ

M_HEADS = 8
M_QK = 128
M_V = 256
A_HEADS = 8
A_QK = 128
A_V = 256
ROPE_DIM = A_QK // 4
ROPE_THETA = 500000.0
D_FF = 5632
CONV_W = 3
EPS = 1e-6

_SPLITS = [M_HEADS * M_QK, M_HEADS * M_QK, M_HEADS * M_V, M_HEADS * M_V, M_HEADS, M_HEADS,
           2 * A_HEADS * A_QK, 2 * A_HEADS * A_QK, A_HEADS * A_V, 2 * D_MODEL]
_OFFS = [0] + [int(o) for o in np.cumsum(_SPLITS)]
GATE_LO, GATE_HI = _OFFS[4], _OFFS[6]
Z_Q, Z_K, Z_V, Z_O = 0, 1024, 2048, 4096
Z_AQ, Z_AK, Z_AV, Z_GT = 6144, 8192, 10240, 12288
N_Z = 16384
LANES = 128

VMEM_LIMIT = 56 * 1024 * 1024

M_CHUNK = 256
A_BLOCK = 512


def _cparams(sem):
    return pltpu.CompilerParams(dimension_semantics=sem, vmem_limit_bytes=VMEM_LIMIT)


def _rmsnorm_rows(x_ref, g_ref, out_ref, rows, chunk=256):
    def body(c, carry):
        r = pl.multiple_of(c * chunk, chunk)
        x = x_ref[pl.ds(r, chunk), :]
        ms = jnp.mean(x * x, axis=-1, keepdims=True)
        out_ref[pl.ds(r, chunk), :] = ((x * lax.rsqrt(ms + EPS)) * g_ref[...]).astype(out_ref.dtype)
        return carry
    lax.fori_loop(0, rows // chunk, body, 0)


_NT_DIMS = (((1,), (1,)), ((), ()))


def _norm_kernel(x_ref, g_ref, out_ref, *, tm):
    _rmsnorm_rows(x_ref, g_ref, out_ref, tm)


def _norm(x, g_all, layer, tm=512):
    row = pl.BlockSpec((tm, D_MODEL), lambda i: (i, 0))
    return pl.pallas_call(
        functools.partial(_norm_kernel, tm=tm),
        out_shape=jax.ShapeDtypeStruct((TOKENS, D_MODEL), BF16),
        grid=(TOKENS // tm,),
        in_specs=[row, pl.BlockSpec((None, 1, D_MODEL), lambda i: (layer, 0, 0))],
        out_specs=row,
        compiler_params=_cparams(("parallel",)),
        name="norm_in",
    )(x, g_all)


def _inproj_kernel(xn_ref, wa_ref, wb_ref, b_ref, c0_ref, c1_ref, c2_ref,
                   z_ref, o0_ref, o1_ref, o2_ref, wbf_ref, *, tm, tn, sub):
    j = pl.program_id(0)
    n_lo = GATE_LO // tn
    shift = GATE_HI - GATE_LO

    @pl.when((pl.program_id(1) == 0) & (j < n_lo))
    def _():
        wbf_ref[...] = wa_ref[...].astype(BF16)

    @pl.when((pl.program_id(1) == 0) & (j >= n_lo))
    def _():
        wbf_ref[0:tn - shift, :] = wa_ref[shift:tn, :].astype(BF16)
        wbf_ref[tn - shift:tn, :] = wb_ref[...].astype(BF16)

    for src, dst in ((c0_ref, o0_ref), (c1_ref, o1_ref), (c2_ref, o2_ref)):
        dst[...] = src[...].astype(BF16)

    for r in range(0, tm, sub):
        z = lax.dot_general(xn_ref[r:r + sub, :], wbf_ref[...], _NT_DIMS, preferred_element_type=F32)
        z_ref[r:r + sub, :] = (z + b_ref[...]).astype(z_ref.dtype)


def _inproj(xn, w_in_t, b_all, w_bm, w_bd, w_o, layer, tm=2048, tn=1024, sub=1024):
    shift = GATE_HI - GATE_LO
    nm = TOKENS // tm
    slab = D_MODEL // ((N_Z // tn) * nm)
    cast_in = pl.BlockSpec((None, slab, D_MODEL), lambda j, i: (layer, j * nm + i, 0))
    cast_out = pl.BlockSpec((slab, D_MODEL), lambda j, i: (j * nm + i, 0))
    wshape = jax.ShapeDtypeStruct((D_MODEL, D_MODEL), BF16)
    return pl.pallas_call(
        functools.partial(_inproj_kernel, tm=tm, tn=tn, sub=sub),
        out_shape=(jax.ShapeDtypeStruct((TOKENS, N_Z), BF16), wshape, wshape, wshape),
        grid=(N_Z // tn, nm),
        in_specs=[
            pl.BlockSpec((tm, D_MODEL), lambda j, i: (i, 0)),
            pl.BlockSpec((None, tn, D_MODEL), lambda j, i: (layer, j, 0)),
            pl.BlockSpec((None, shift, D_MODEL), lambda j, i: (layer, (j + 1) * (tn // shift), 0)),
            pl.BlockSpec((None, 1, tn), lambda j, i: (layer, 0, j)),
            cast_in, cast_in, cast_in,
        ],
        out_specs=(pl.BlockSpec((tm, tn), lambda j, i: (i, j)), cast_out, cast_out, cast_out),
        scratch_shapes=[pltpu.VMEM((tn, D_MODEL), BF16)],
        compiler_params=_cparams(("arbitrary", "arbitrary")),
        name="inproj",
    )(xn, w_in_t, w_in_t, b_all, w_bm, w_bd, w_o)


def _log_sigmoid(x):
    return -(jnp.maximum(-x, 0.0) + jnp.log1p(jnp.exp(-jnp.abs(x))))


def _cummax_rows(x, rows):
    ridx = lax.broadcasted_iota(jnp.int32, x.shape, 0)
    k = 1
    while k < 8:
        x = jnp.maximum(x, jnp.where(ridx < k, -jnp.inf, pltpu.roll(x, k, 0)))
        k *= 2
    while k < rows:
        pad = jnp.full((k, x.shape[1]), -jnp.inf, x.dtype)
        x = jnp.maximum(x, jnp.concatenate([pad, x[:rows - k]], axis=0))
        k *= 2
    return x


def _mlstm_kernel(q_ref, k_ref, v_ref, o_ref, xn_ref, wgt_ref, bg_ref, mn_ref, out_ref,
                  ct_ref, m_ref, *, L):
    @pl.when(pl.program_id(1) == 0)
    def _():
        ct_ref[...] = jnp.zeros_like(ct_ref)
        m_ref[...] = jnp.zeros_like(m_ref)

    scale = M_QK ** -0.5
    log2e = math.log2(math.e)
    row = lax.broadcasted_iota(jnp.int32, (L, L), 0)
    col = lax.broadcasted_iota(jnp.int32, (L, L), 1)
    causal = row >= col
    zg = lax.dot_general(xn_ref[...], wgt_ref[...].astype(BF16), _NT_DIMS,
                         preferred_element_type=F32)
    z_i = zg + bg_ref[:, 0:LANES]
    z_f = pltpu.roll(zg, LANES - M_HEADS, 1) + bg_ref[:, LANES:2 * LANES]
    lf = _log_sigmoid(z_f)
    b = jnp.dot(causal.astype(F32), lf, precision=lax.Precision.HIGHEST,
                preferred_element_type=F32)
    a = z_i - b
    m_prev = m_ref[0:1, :]
    big_m = jnp.maximum(m_prev, _cummax_rows(a, L))
    a2_t = (a * log2e).T
    m2 = big_m * log2e - math.log2(scale)
    e_inter = jnp.exp(m_prev - big_m) * scale
    e_negm = jnp.exp(-(b + big_m))
    m_last = big_m[L - 1:L, :]
    w_all = jnp.exp(a - m_last)
    decay = jnp.exp(m_prev - m_last)
    m_ref[0:1, :] = b[L - 1:L, :] + m_last
    ones_blk = jnp.ones((L, LANES), BF16)
    tn_dims = (((0,), (0,)), ((), ()))

    for h in range(M_HEADS):
        q = q_ref[:, h * M_QK:(h + 1) * M_QK]
        k = k_ref[:, h * M_QK:(h + 1) * M_QK]
        v = v_ref[:, h * M_V:(h + 1) * M_V]
        m2_b = jnp.broadcast_to(m2[:, h:h + 1], (L, LANES))
        arg = jnp.where(causal, a2_t[h:h + 1, :] - _lane_tile(m2_b, L), -jnp.inf)
        qk = lax.dot_general(q, k, _NT_DIMS, preferred_element_type=F32)
        s_mat = qk * jnp.exp2(arg)
        ct = ct_ref[h]
        qc = jnp.dot(q, ct.astype(BF16), preferred_element_type=F32)
        inter = e_inter[:, h:h + 1]
        num = (jnp.dot(s_mat.astype(BF16), v, preferred_element_type=F32)
               + jnp.broadcast_to(inter, (L, M_V)) * qc[:, :M_V])
        den = jnp.sum(s_mat, axis=1, keepdims=True) + inter * qc[:, M_V:M_V + 1]
        rden = 1.0 / jnp.maximum(jnp.abs(den), e_negm[:, h:h + 1])
        hh = num * rden
        ms = jnp.mean(hh * hh, axis=-1, keepdims=True)
        hn = (hh * lax.rsqrt(ms + EPS)) * mn_ref[:, h * M_V:(h + 1) * M_V]
        og = o_ref[:, h * M_V:(h + 1) * M_V].astype(F32)
        out_ref[:, h * M_V:(h + 1) * M_V] = (hn * jax.nn.sigmoid(og)).astype(out_ref.dtype)

        kw = (jnp.broadcast_to(w_all[:, h:h + 1], (L, M_QK)) * k.astype(F32)).astype(BF16)
        upd = jnp.concatenate(
            [lax.dot_general(kw, v, tn_dims, preferred_element_type=F32),
             lax.dot_general(kw, ones_blk, tn_dims, preferred_element_type=F32)], axis=1)
        ct_ref[h] = decay[:, h:h + 1] * ct + upd


def _mlstm(z, xn, w_in_t, bg_all, m_norm_all, layer, L=M_CHUNK):
    nc = SEQ // L
    row = lambda b, c: b * nc + c
    return pl.pallas_call(
        functools.partial(_mlstm_kernel, L=L),
        out_shape=jax.ShapeDtypeStruct((TOKENS, M_HEADS * M_V), BF16),
        grid=(BATCH, nc),
        in_specs=[
            pl.BlockSpec((L, 1024), lambda b, c: (row(b, c), Z_Q // 1024)),
            pl.BlockSpec((L, 1024), lambda b, c: (row(b, c), Z_K // 1024)),
            pl.BlockSpec((L, 2048), lambda b, c: (row(b, c), Z_V // 2048)),
            pl.BlockSpec((L, 2048), lambda b, c: (row(b, c), Z_O // 2048)),
            pl.BlockSpec((L, D_MODEL), lambda b, c: (row(b, c), 0)),
            pl.BlockSpec((None, LANES, D_MODEL), lambda b, c: (layer, GATE_LO // LANES, 0)),
            pl.BlockSpec((None, 1, 2 * LANES), lambda b, c: (layer, 0, 0)),
            pl.BlockSpec((None, 1, M_HEADS * M_V), lambda b, c: (layer, 0, 0)),
        ],
        out_specs=pl.BlockSpec((L, M_HEADS * M_V), lambda b, c: (row(b, c), 0)),
        scratch_shapes=[pltpu.VMEM((M_HEADS, M_QK, M_V + LANES), F32),
                        pltpu.VMEM((M_HEADS, LANES), F32)],
        compiler_params=_cparams(("parallel", "arbitrary")),
        name="mlstm",
    )(z, z, z, z, xn, w_in_t, bg_all, m_norm_all)


def _rope(x, cos, sin):
    half = ROPE_DIM // 2
    lane = lax.broadcasted_iota(jnp.int32, x.shape, 1)
    partner = jnp.where(lane < half, pltpu.roll(x, LANES - half, 1), pltpu.roll(x, half, 1))
    return x * cos + partner * sin


def _lane_groups(x, op):
    acc = x[:, 0:LANES]
    for g in range(1, x.shape[1] // LANES):
        acc = op(acc, x[:, g * LANES:(g + 1) * LANES])
    return acc


def _lane_tile(a, width):
    return jnp.concatenate([a] * (width // LANES), axis=1) if width > LANES else a


def _attn_kernel(q_ref, k_ref, v_ref, cos_ref, sin_ref, lq_ref, li_ref, an_ref, out_ref,
                 kr_ref, s_ref, a_ref, *, tq, nh):
    for hh in range(nh):
        qk_cols = slice(hh * 2 * A_QK, (hh + 1) * 2 * A_QK)
        v_cols = slice(hh * A_V, (hh + 1) * A_V)
        _attn_head(q_ref.at[:, qk_cols], k_ref.at[:, qk_cols], v_ref.at[:, v_cols], cos_ref, sin_ref,
                   lq_ref, li_ref, an_ref.at[:, v_cols], out_ref.at[:, v_cols], kr_ref.at[hh],
                   s_ref, a_ref, tq=tq)


def _attn_head(q_ref, k_ref, v_ref, cos_ref, sin_ref, lq_ref, li_ref, an_ref, out_ref,
               kr_ref, s_ref, a_ref, *, tq):
    qscale = (A_QK ** -0.5) * math.log2(math.e)
    nt = (((1,), (1,)), ((), ()))
    rchunk = 256

    def rope_keys(r0, rows):
        for r in range(r0, r0 + rows, rchunk):
            kk = k_ref[r:r + rchunk, :].astype(F32)
            cos = cos_ref[r:r + rchunk, :]
            sin = sin_ref[r:r + rchunk, :]
            for mp in range(2):
                kr_ref[r:r + rchunk, mp * A_QK:(mp + 1) * A_QK] = _rope(
                    kk[:, mp * A_QK:(mp + 1) * A_QK], cos, sin).astype(BF16)

    lq = lq_ref[...]
    lam_init = li_ref[0:1, 0:1]
    lam = (jnp.exp(jnp.sum(lq[0:1] * lq[1:2], axis=1, keepdims=True))
           - jnp.exp(jnp.sum(lq[2:3] * lq[3:4], axis=1, keepdims=True)) + lam_init)
    causal = (lax.broadcasted_iota(jnp.int32, (tq, tq), 0)
              >= lax.broadcasted_iota(jnp.int32, (tq, tq), 1))

    nq = SEQ // tq

    def scores(t):
        r0 = t * tq
        rope_keys(r0, tq)
        qf = q_ref[r0:r0 + tq, :].astype(F32)
        cos_q = cos_ref[r0:r0 + tq, :]
        sin_q = sin_ref[r0:r0 + tq, :]
        for mp in range(2):
            qr = (_rope(qf[:, mp * A_QK:(mp + 1) * A_QK], cos_q, sin_q) * qscale).astype(BF16)
            s_ref[t % 2, mp, :, 0:r0 + tq] = lax.dot_general(
                qr, kr_ref[0:r0 + tq, mp * A_QK:(mp + 1) * A_QK], nt, preferred_element_type=F32)

    scores(0)
    for t in range(nq):
        if t + 1 < nq:
            scores(t + 1)
        r0 = t * tq
        buf = t % 2
        lsum = []
        for mp in range(2):
            s_d = jnp.where(causal, s_ref[buf, mp, :, r0:r0 + tq], -jnp.inf)
            mc = _lane_groups(s_d, jnp.maximum)
            if t:
                mc = jnp.maximum(mc, _lane_groups(s_ref[buf, mp, :, 0:r0], jnp.maximum))
            m = jnp.broadcast_to(jnp.max(mc, axis=1, keepdims=True), (tq, LANES))
            p_d = jnp.exp2(s_d - _lane_tile(m, tq))
            ps = _lane_groups(p_d, jnp.add)
            s_ref[buf, mp, :, r0:r0 + tq] = p_d
            if t:
                p_f = jnp.exp2(s_ref[buf, mp, :, 0:r0] - _lane_tile(m, r0))
                ps = ps + _lane_groups(p_f, jnp.add)
                s_ref[buf, mp, :, 0:r0] = p_f
            lsum.append(jnp.sum(ps, axis=1, keepdims=True))
        ratio = jnp.broadcast_to(lam * lsum[0] / lsum[1], (tq, LANES))
        kv = r0 + tq
        a_ref[buf, :, 0:kv] = (s_ref[buf, 0, :, 0:kv]
                               - s_ref[buf, 1, :, 0:kv] * _lane_tile(ratio, kv)).astype(BF16)
        o = jnp.dot(a_ref[buf, :, 0:kv], v_ref[0:kv, :], preferred_element_type=F32)
        o = o / lsum[0]
        ms = jnp.mean(o * o, axis=-1, keepdims=True)
        on = ((o * lax.rsqrt(ms + EPS)) * an_ref[...]) * (1.0 - lam_init)
        out_ref[r0:r0 + tq, :] = on.astype(out_ref.dtype)


def _attn(z, cos_t, sin_t, lam_qk, lam_init_arr, a_norm_all, layer, tq=A_BLOCK, nh=2):
    wq = nh * 2 * A_QK
    wv = nh * A_V
    return pl.pallas_call(
        functools.partial(_attn_kernel, tq=tq, nh=nh),
        out_shape=jax.ShapeDtypeStruct((TOKENS, A_HEADS * A_V), BF16),
        grid=(BATCH, A_HEADS // nh),
        in_specs=[
            pl.BlockSpec((SEQ, wq), lambda b, h: (b, Z_AQ // wq + h)),
            pl.BlockSpec((SEQ, wq), lambda b, h: (b, Z_AK // wq + h)),
            pl.BlockSpec((SEQ, wv), lambda b, h: (b, Z_AV // wv + h)),
            pl.BlockSpec((SEQ, LANES), lambda b, h: (0, 0)),
            pl.BlockSpec((SEQ, LANES), lambda b, h: (0, 0)),
            pl.BlockSpec((None, 4, A_QK), lambda b, h: (layer, 0, 0)),
            pl.BlockSpec((None, 1, LANES), lambda b, h: (layer, 0, 0)),
            pl.BlockSpec((None, 1, wv), lambda b, h: (layer, 0, h)),
        ],
        out_specs=pl.BlockSpec((SEQ, wv), lambda b, h: (b, h)),
        scratch_shapes=[pltpu.VMEM((nh, SEQ, 2 * A_QK), BF16),
                        pltpu.VMEM((2, 2, tq, SEQ), F32),
                        pltpu.VMEM((2, tq, SEQ), BF16)],
        compiler_params=_cparams(("parallel", "parallel")),
        name="diff_attn",
    )(z, z, z, cos_t, sin_t, lam_qk, lam_init_arr, a_norm_all)


def _mixout_kernel(hm_ref, ha_ref, gm_ref, gd_ref, wm_ref, wd_ref, wo_ref, h_ref, g_ref,
                   out_ref, xn_ref, y_ref, *, tm, nsplit):
    cw = D_MODEL // nsplit
    hm = hm_ref[...]
    ha = ha_ref[...]
    for n in range(nsplit):
        c = slice(n * cw, (n + 1) * cw)
        ym = jnp.dot(hm, wm_ref[:, c], preferred_element_type=F32)
        yd = jnp.dot(ha, wd_ref[:, c], preferred_element_type=F32)
        y = (jax.nn.sigmoid(gm_ref[:, c].astype(F32)) * ym
             + jax.nn.sigmoid(gd_ref[:, c].astype(F32)) * yd)
        y_ref[:, c] = y.astype(y_ref.dtype)
    y = y_ref[...]
    for n in range(nsplit):
        c = slice(n * cw, (n + 1) * cw)
        out_ref[:, c] = h_ref[:, c] + jnp.dot(y, wo_ref[:, c], preferred_element_type=F32)
    _rmsnorm_rows(out_ref, g_ref, xn_ref, tm, chunk=tm)


def _mixout(h, hm, ha, z, wbm, wbd, wo, g_all, layer, tm=256, nsplit=4):
    wspec = pl.BlockSpec((D_MODEL, D_MODEL), lambda i: (0, 0), pipeline_mode=pl.Buffered(1))
    row = pl.BlockSpec((tm, D_MODEL), lambda i: (i, 0))
    return pl.pallas_call(
        functools.partial(_mixout_kernel, tm=tm, nsplit=nsplit),
        out_shape=(jax.ShapeDtypeStruct((TOKENS, D_MODEL), F32),
                   jax.ShapeDtypeStruct((TOKENS, D_MODEL), BF16)),
        grid=(TOKENS // tm,),
        in_specs=[
            row, row,
            pl.BlockSpec((tm, D_MODEL), lambda i: (i, Z_GT // D_MODEL)),
            pl.BlockSpec((tm, D_MODEL), lambda i: (i, Z_GT // D_MODEL + 1)),
            wspec, wspec, wspec,
            row,
            pl.BlockSpec((None, 1, D_MODEL), lambda i: (layer, 0, 0)),
        ],
        out_specs=(row, row),
        scratch_shapes=[pltpu.VMEM((tm, D_MODEL), BF16)],
        compiler_params=_cparams(("parallel",)),
        name="mix_out",
    )(hm, ha, z, z, wbm, wbd, wo, h, g_all)


def _ffn_up_kernel(xn_ref, wg_ref, wv_ref, cw_ref, cb_ref, wd_ref, u_ref, wdb_ref,
                   wgb_ref, wvb_ref, carry_ref, *, tm, tn, sub):
    i = pl.program_id(1)
    wdb_ref[...] = wd_ref[...].astype(BF16)

    @pl.when(i == 0)
    def _():
        wgb_ref[...] = wg_ref[...].astype(BF16)
        wvb_ref[...] = wv_ref[...].astype(BF16)

    @pl.when((i % (SEQ // tm)) == 0)
    def _():
        carry_ref[...] = jnp.zeros_like(carry_ref)

    prev = carry_ref[...]
    cw = cw_ref[...]
    cb = cb_ref[...]
    rowi = lax.broadcasted_iota(jnp.int32, (sub, tn), 0)
    is0 = rowi == 0
    is1 = rowi == 1
    for r in range(tm // sub):
        xs = xn_ref[r * sub:(r + 1) * sub, :]
        gate = jnp.dot(xs, wgb_ref[...], preferred_element_type=F32)
        val = jnp.dot(xs, wvb_ref[...], preferred_element_type=F32)
        g1 = jnp.where(is0, prev[7:8, :], pltpu.roll(gate, 1, 0))
        g2 = jnp.where(is0, prev[6:7, :], jnp.where(is1, prev[7:8, :], pltpu.roll(gate, 2, 0)))
        conv = cw[0:1, :] * g2 + cw[1:2, :] * g1 + cw[2:3, :] * gate + cb
        u_ref[r * sub:(r + 1) * sub, :] = ((conv * jax.nn.sigmoid(conv)) * val).astype(u_ref.dtype)
        prev = gate[sub - 8:sub, :]
    carry_ref[...] = prev


def _ffn_up(xn, wup_all, cw_all, cb_all, wdown_all, layer, tm=2048, tn=512, sub=1024):
    nf = D_FF // tn
    nm = TOKENS // tm
    slab = D_FF // (nf * nm)
    return pl.pallas_call(
        functools.partial(_ffn_up_kernel, tm=tm, tn=tn, sub=sub),
        out_shape=(jax.ShapeDtypeStruct((TOKENS, D_FF), BF16),
                   jax.ShapeDtypeStruct((D_FF, D_MODEL), BF16)),
        grid=(nf, nm),
        in_specs=[
            pl.BlockSpec((tm, D_MODEL), lambda j, i: (i, 0)),
            pl.BlockSpec((None, D_MODEL, tn), lambda j, i: (layer, 0, j)),
            pl.BlockSpec((None, D_MODEL, tn), lambda j, i: (layer, 0, nf + j)),
            pl.BlockSpec((None, CONV_W, tn), lambda j, i: (layer, 0, j)),
            pl.BlockSpec((None, 1, tn), lambda j, i: (layer, 0, j)),
            pl.BlockSpec((None, slab, D_MODEL), lambda j, i: (layer, j * nm + i, 0)),
        ],
        out_specs=(pl.BlockSpec((tm, tn), lambda j, i: (i, j)),
                   pl.BlockSpec((slab, D_MODEL), lambda j, i: (j * nm + i, 0))),
        scratch_shapes=[pltpu.VMEM((D_MODEL, tn), BF16),
                        pltpu.VMEM((D_MODEL, tn), BF16),
                        pltpu.VMEM((8, tn), F32)],
        compiler_params=_cparams(("arbitrary", "arbitrary")),
        name="ffn_up",
    )(xn, wup_all, wup_all, cw_all, cb_all, wdown_all)


def _ffn_down_kernel(u_ref, w_ref, h_ref, g_ref, *out_refs, tm, nsplit, last):
    cw = D_MODEL // nsplit
    u = u_ref[...]
    if last:
        (y_ref,) = out_refs
        hn_ref = y_ref
    else:
        hn_ref, y_ref = out_refs
    for n in range(nsplit):
        c = slice(n * cw, (n + 1) * cw)
        hn_ref[:, c] = h_ref[:, c] + jnp.dot(u, w_ref[:, c], preferred_element_type=F32)
    _rmsnorm_rows(hn_ref, g_ref, y_ref, tm, chunk=tm)


def _ffn_down(h, u, w, g, last, tm=256, nsplit=4):
    row = pl.BlockSpec((tm, D_MODEL), lambda i: (i, 0))
    if last:
        out_shape = jax.ShapeDtypeStruct((TOKENS, D_MODEL), F32)
        out_specs = row
    else:
        out_shape = (jax.ShapeDtypeStruct((TOKENS, D_MODEL), F32),
                     jax.ShapeDtypeStruct((TOKENS, D_MODEL), BF16))
        out_specs = (row, row)
    return pl.pallas_call(
        functools.partial(_ffn_down_kernel, tm=tm, nsplit=nsplit, last=last),
        out_shape=out_shape,
        grid=(TOKENS // tm,),
        in_specs=[
            pl.BlockSpec((tm, D_FF), lambda i: (i, 0)),
            pl.BlockSpec((D_FF, D_MODEL), lambda i: (0, 0), pipeline_mode=pl.Buffered(1)),
            row,
            pl.BlockSpec((1, D_MODEL), lambda i: (0, 0)),
        ],
        out_specs=out_specs,
        compiler_params=_cparams(("parallel",)),
        name="ffn_down",
    )(u, w, h, g)


def _rope_tables():
    half = ROPE_DIM // 2
    inv = ROPE_THETA ** (-jnp.arange(half, dtype=F32) / half)
    ang = jnp.arange(SEQ).astype(F32)[:, None] * inv[None, :]
    cos, sin = jnp.cos(ang), jnp.sin(ang)
    ones = jnp.ones((SEQ, LANES - ROPE_DIM), F32)
    zeros = jnp.zeros((SEQ, LANES - ROPE_DIM), F32)
    return (jnp.concatenate([cos, cos, ones], axis=1),
            jnp.concatenate([-sin, sin, zeros], axis=1))


def kernel(x, norm_mix, w_in, b_in, m_norm, a_norm, lam_qk, w_bm, w_bd, w_o, norm_ffn, w_up, conv_w, conv_b, w_down, norm_final):
    w_in_t = jnp.swapaxes(w_in, 1, 2)
    b_main = jnp.concatenate([b_in[:, :GATE_LO], b_in[:, GATE_HI:]], axis=1)[:, None, :]
    pad = ((0, 0), (0, LANES - M_HEADS))
    b_gate = jnp.concatenate([jnp.pad(b_in[:, GATE_LO:GATE_LO + M_HEADS], pad),
                              jnp.pad(b_in[:, GATE_LO + M_HEADS:GATE_HI], pad)], axis=1)[:, None, :]
    norm_mix3, norm_ffn3 = norm_mix[:, None, :], norm_ffn[:, None, :]
    m_norm3, a_norm3 = m_norm[:, None, :], a_norm[:, None, :]
    conv_b3 = conv_b[:, None, :]
    lam_init = jnp.asarray([0.8 - 0.6 * math.exp(-0.3 * l) for l in range(DEPTH)], F32)
    lam_init3 = jnp.broadcast_to(lam_init[:, None, None], (DEPTH, 1, LANES))
    cos_t, sin_t = _rope_tables()

    h = x.reshape(TOKENS, D_MODEL)
    xn = _norm(h, norm_mix3, 0)
    for l in range(DEPTH):
        last = l == DEPTH - 1
        z, w_bm_b, w_bd_b, w_o_b = _inproj(xn, w_in_t, b_main, w_bm, w_bd, w_o, l)
        hm = _mlstm(z, xn, w_in_t, b_gate, m_norm3, l)
        ha = _attn(z, cos_t, sin_t, lam_qk, lam_init3, a_norm3, l)
        h, xn = _mixout(h, hm, ha, z, w_bm_b, w_bd_b, w_o_b, norm_ffn3, l)
        u, w_down_b = _ffn_up(xn, w_up, conv_w, conv_b3, w_down, l)
        if last:
            out = _ffn_down(h, u, w_down_b, norm_final[None, :], True)
        else:
            h, xn = _ffn_down(h, u, w_down_b, norm_mix[l + 1][None, :], False)
    return out.reshape(BATCH, SEQ, D_MODEL)
```
